```python
import jax, jax.numpy as jnp
from jax import lax
import numpy as np

D_MODEL = 1024
BATCH = 8
SEQ = 4096
DEPTH = 2
DEC_BATCH = 8
DEC_SEQ = 16
PAST_LEN = 1024

CHUNK = 64
Q_BLOCK = 128
D_HEAD = 64
H_A = 8
H_B = 8
H_C = 8
W_BR = 512
DKV_B = 64
H_IDX = 4
D_IDX = 64
TOPK_MAX = 256
BAND_CHUNKS = 8
BAND_ROWS = BAND_CHUNKS * CHUNK
REL_CLIP = 128
N_BRANCH = 3
DEEPNORM_ALPHA = (2.0 * DEPTH) ** 0.25
DEEPNORM_BETA = (8.0 * DEPTH) ** -0.25
LN_EPS = 1e-5
NEG_INF = -1e30

IN_SIZES = (W_BR, W_BR, W_BR, W_BR,
            W_BR, DKV_B, DKV_B, W_BR,
            H_IDX * D_IDX, D_IDX, H_IDX,
            W_BR, W_BR, W_BR, W_BR)
N_IN = sum(IN_SIZES)
IN_SPLITS = tuple(sum(IN_SIZES[:i + 1]) for i in range(len(IN_SIZES) - 1))

kernel_name = "hybrid_streaming_encoder_step"


def layer_norm(x, g=None, b=None):
    xf = x.astype(jnp.float32)
    mu = xf.mean(-1, keepdims=True)
    var = jnp.square(xf - mu).mean(-1, keepdims=True)
    y = (xf - mu) * lax.rsqrt(var + LN_EPS)
    if g is not None:
        y = y * g.astype(jnp.float32) + b.astype(jnp.float32)
    return y.astype(x.dtype)


def alibi_slopes():
    return 2.0 ** (-8.0 * jnp.arange(1, H_B + 1, dtype=jnp.float32) / H_B)


def to_blocks(x, size):
    b, t = x.shape[:2]
    return jnp.moveaxis(x.reshape(b, t // size, size, *x.shape[2:]), 1, 0)


def from_blocks(x):
    nb, b, size = x.shape[:3]
    return jnp.moveaxis(x, 0, 1).reshape(b, nb * size, *x.shape[3:])


def stick_breaking(q, k, v, q_pos, k_pos):
    z = jnp.einsum('bqhd,bkhd->bhqk', q, k).astype(jnp.float32) * (D_HEAD ** -0.5)
    valid = k_pos[None, :] < q_pos[:, None]
    log_1mb = jnp.where(valid, jax.nn.log_sigmoid(-z), 0.0)
    rc = lax.cumsum(log_1mb, axis=3, reverse=True)
    after = jnp.pad(rc[..., 1:], ((0, 0), (0, 0), (0, 0), (0, 1)))
    a = jnp.where(valid, jnp.exp(jax.nn.log_sigmoid(z) + after), 0.0)
    return jnp.einsum('bhqk,bkhd->bqhd', a.astype(v.dtype), v)


def dsa_attend(q, k, v, iq, ik, iw, q_pos, k_pos, n_sel):
    adm = (k_pos // CHUNK)[None, :] <= (q_pos // CHUNK)[:, None]
    logit = jnp.einsum('bqhd,bkd->bqhk', iq, ik).astype(jnp.float32) * (D_IDX ** -0.5)
    score = jnp.einsum('bqh,bqhk->bqk', iw.astype(jnp.float32), jax.nn.relu(logit)) * (H_IDX ** -0.5)
    score = jnp.where(adm[None], score, NEG_INF)
    _, sel = lax.top_k(score, n_sel)
    sel_pos = k_pos[sel]
    valid = (sel_pos // CHUNK) <= (q_pos // CHUNK)[None, :, None]
    gather = jax.vmap(lambda rows, idx: rows[idx])
    k_sel = gather(k, sel)
    v_sel = gather(v, sel)
    dist = jnp.abs(q_pos[None, :, None] - sel_pos).astype(jnp.float32)
    s = (jnp.einsum('bqhd,bqnd->bhqn', q, k_sel).astype(jnp.float32) * (D_HEAD ** -0.5)
         - alibi_slopes()[None, :, None, None] * dist[:, None])
    s = jnp.where(valid[:, None], s, NEG_INF)
    p = jax.nn.softmax(s, axis=-1)
    return jnp.einsum('bhqn,bqnd->bqhd', p.astype(v.dtype), v_sel)


def band_attend(q, k, v, q_pos, k_pos, rel_table):
    qc = q_pos // CHUNK
    kc = k_pos // CHUNK
    valid = ((k_pos[None, :] >= 0) & (kc[None, :] <= qc[:, None])
             & (kc[None, :] >= qc[:, None] - BAND_CHUNKS))
    rel = jnp.clip(q_pos[:, None] - k_pos[None, :], -REL_CLIP, REL_CLIP) + REL_CLIP
    bias = jnp.take(rel_table, rel, axis=1).astype(jnp.float32)
    s = jnp.einsum('bqhd,bkhd->bhqk', q, k).astype(jnp.float32) * (D_HEAD ** -0.5) + bias[None]
    s = jnp.where(valid[None, None], s, NEG_INF)
    p = jax.nn.softmax(s, axis=-1)
    return jnp.einsum('bhqk,bkhd->bqhd', p.astype(v.dtype), v)


def prompt_stick_breaking(q, k, v):
    pos = jnp.arange(q.shape[1])
    out = lax.map(lambda xs: stick_breaking(xs[0], k, v, xs[1], pos),
                  (to_blocks(q, Q_BLOCK), pos.reshape(-1, Q_BLOCK)))
    return from_blocks(out)


def prompt_dsa(q, k, v, iq, ik, iw):
    t = q.shape[1]
    pos = jnp.arange(t)
    n_sel = min(TOPK_MAX, t // 4)
    out = lax.map(lambda xs: dsa_attend(xs[0], k, v, xs[1], ik, xs[2], xs[3], pos, n_sel),
                  (to_blocks(q, Q_BLOCK), to_blocks(iq, Q_BLOCK), to_blocks(iw, Q_BLOCK),
                   pos.reshape(-1, Q_BLOCK)))
    return from_blocks(out)


def prompt_band(q, k, v, rel_table):
    n_chunks = q.shape[1] // CHUNK
    kp = jnp.pad(k, ((0, 0), (BAND_ROWS, 0), (0, 0), (0, 0)))
    vp = jnp.pad(v, ((0, 0), (BAND_ROWS, 0), (0, 0), (0, 0)))

    def one_chunk(xs):
        qc, j = xs
        start = j * CHUNK
        kb = lax.dynamic_slice_in_dim(kp, start, BAND_ROWS + CHUNK, axis=1)
        vb = lax.dynamic_slice_in_dim(vp, start, BAND_ROWS + CHUNK, axis=1)
        q_pos = start + jnp.arange(CHUNK)
        k_pos = start - BAND_ROWS + jnp.arange(BAND_ROWS + CHUNK)
        return band_attend(qc, kb, vb, q_pos, k_pos, rel_table)

    out = lax.map(one_chunk, (to_blocks(q, CHUNK), jnp.arange(n_chunks)))
    return from_blocks(out)


def modulate(x, c, w_ada_l, b_ada_l):
    mod = jax.nn.silu(c) @ w_ada_l + b_ada_l
    shift, scale, gate = jnp.split(mod, 3, axis=-1)
    h = layer_norm(x) * (1.0 + scale[:, None]) + shift[:, None]
    return h, gate


def in_proj(h, w_in_l):
    b, t = h.shape[:2]
    (aq, ak, av, az, bq, bk, bv, bz, biq, bik, biw, cq, ck, cv, cz) = jnp.split(h @ w_in_l, IN_SPLITS, axis=-1)
    heads = lambda y, n: y.reshape(b, t, n, -1)
    return (heads(aq, H_A), heads(ak, H_A), heads(av, H_A), az,
            heads(bq, H_B), bk, bv, bz, heads(biq, H_IDX), bik, biw,
            heads(cq, H_C), heads(ck, H_C), heads(cv, H_C), cz)


def merge(x, h, gate_c, branch_outs, gate_paths, w_gate_l, b_gate_l, w_branch_l, w_out_l, ln_g_l, ln_b_l):
    b, t = x.shape[:2]
    m = 0.0
    for n in range(N_BRANCH):
        y = branch_outs[n].reshape(b, t, W_BR) * jax.nn.silu(gate_paths[n])
        g = jax.nn.sigmoid(h @ w_gate_l[n] + b_gate_l[n])
        m = m + g * (y @ w_branch_l[n])
    out = m @ w_out_l
    return layer_norm(DEEPNORM_ALPHA * x + gate_c[:, None] * out, ln_g_l, ln_b_l)


def setup_inputs(seed: int = 0) -> dict:
    key = jax.random.key(seed)
    ks = jax.random.split(key, 24)
    nrm = lambda k, shape, s: jax.random.normal(k, shape, jnp.float32) * s
    band_rows = min(BAND_ROWS, PAST_LEN)
    d = D_MODEL
    return {
        "x_prompt": nrm(ks[0], (BATCH, SEQ, d), 1.0),
        "x_sample": nrm(ks[1], (DEC_BATCH, DEC_SEQ, d), 1.0),
        "c_prompt": nrm(ks[2], (BATCH, d), 1.0),
        "c_sample": nrm(ks[3], (DEC_BATCH, d), 1.0),
        "cache_a_k": nrm(ks[4], (DEPTH, DEC_BATCH, PAST_LEN, H_A, D_HEAD), 1.0),
        "cache_a_v": nrm(ks[5], (DEPTH, DEC_BATCH, PAST_LEN, H_A, D_HEAD), 1.0),
        "cache_b_k": nrm(ks[6], (DEPTH, DEC_BATCH, PAST_LEN, DKV_B), 1.0),
        "cache_b_v": nrm(ks[7], (DEPTH, DEC_BATCH, PAST_LEN, DKV_B), 1.0),
        "cache_b_kidx": nrm(ks[8], (DEPTH, DEC_BATCH, PAST_LEN, D_IDX), 1.0),
        "cache_c_k": nrm(ks[9], (DEPTH, DEC_BATCH, band_rows, H_C, D_HEAD), 1.0),
        "cache_c_v": nrm(ks[10], (DEPTH, DEC_BATCH, band_rows, H_C, D_HEAD), 1.0),
        "w_ada": nrm(ks[11], (DEPTH, d, 3 * d), d ** -0.5),
        "b_ada": nrm(ks[12], (DEPTH, 3 * d), 0.02),
        "w_in": nrm(ks[13], (DEPTH, d, N_IN), d ** -0.5),
        "w_gate": nrm(ks[14], (DEPTH, N_BRANCH, d, d), d ** -0.5),
        "b_gate": nrm(ks[15], (DEPTH, N_BRANCH, d), 0.02),
        "w_branch": nrm(ks[16], (DEPTH, N_BRANCH, W_BR, d), DEEPNORM_BETA * W_BR ** -0.5),
        "w_out": nrm(ks[17], (DEPTH, d, d), DEEPNORM_BETA * d ** -0.5),
        "rel_bias": nrm(ks[18], (DEPTH, H_C, 2 * REL_CLIP + 1), 0.1),
        "ln_g": 1.0 + nrm(ks[19], (DEPTH, d), 0.02),
        "ln_b": nrm(ks[20], (DEPTH, d), 0.02),
    }


def reference(x_prompt, x_sample, c_prompt, c_sample, cache_a_k, cache_a_v, cache_b_k, cache_b_v,
              cache_b_kidx, cache_c_k, cache_c_v, w_ada, b_ada, w_in, w_gate, b_gate, w_branch,
              w_out, rel_bias, ln_g, ln_b):
    past_len = cache_a_k.shape[2]
    dec_seq = x_sample.shape[1]
    band_rows = cache_c_k.shape[2]
    q_pos_s = past_len + jnp.arange(dec_seq)
    k_pos_full = jnp.arange(past_len + dec_seq)
    k_pos_band = jnp.arange(past_len - band_rows, past_len + dec_seq)
    n_sel_s = min(TOPK_MAX, (past_len + dec_seq) // 4)

    names = ("a_k", "a_v", "b_k", "b_v", "b_kidx", "c_k", "c_v")
    st_p = {n: [] for n in names}
    st_s = {n: [] for n in names}
    xp, xs = x_prompt, x_sample
    for l in range(DEPTH):
        h, gate_c = modulate(xp, c_prompt, w_ada[l], b_ada[l])
        (aq, ak, av, az, bq, bk, bv, bz, biq, bik, biw, cq, ck, cv, cz) = in_proj(h, w_in[l])
        ya = prompt_stick_breaking(aq, ak, av)
        yb = prompt_dsa(bq, bk, bv, biq, bik, biw)
        yc = prompt_band(cq, ck, cv, rel_bias[l])
        xp = merge(xp, h, gate_c, (ya, yb, yc), (az, bz, cz), w_gate[l], b_gate[l], w_branch[l],
                   w_out[l], ln_g[l], ln_b[l])
        for n, val in zip(names, (ak, av, bk, bv, bik, ck[:, -BAND_ROWS:], cv[:, -BAND_ROWS:])):
            st_p[n].append(val)

        h, gate_c = modulate(xs, c_sample, w_ada[l], b_ada[l])
        (aq, ak, av, az, bq, bk, bv, bz, biq, bik, biw, cq, ck, cv, cz) = in_proj(h, w_in[l])
        ka = jnp.concatenate([cache_a_k[l], ak], axis=1)
        va = jnp.concatenate([cache_a_v[l], av], axis=1)
        kb = jnp.concatenate([cache_b_k[l], bk], axis=1)
        vb = jnp.concatenate([cache_b_v[l], bv], axis=1)
        kidx = jnp.concatenate([cache_b_kidx[l], bik], axis=1)
        kc = jnp.concatenate([cache_c_k[l], ck], axis=1)
        vc = jnp.concatenate([cache_c_v[l], cv], axis=1)
        ya = stick_breaking(aq, ka, va, q_pos_s, k_pos_full)
        yb = dsa_attend(bq, kb, vb, biq, kidx, biw, q_pos_s, k_pos_full, n_sel_s)
        yc = band_attend(cq, kc, vc, q_pos_s, k_pos_band, rel_bias[l])
        xs = merge(xs, h, gate_c, (ya, yb, yc), (az, bz, cz), w_gate[l], b_gate[l], w_branch[l],
                   w_out[l], ln_g[l], ln_b[l])
        for n, val in zip(names, (ak, av, bk, bv, bik, kc[:, -band_rows:], vc[:, -band_rows:])):
            st_s[n].append(val)

    y_prompt, y_sample = xp, xs
    new_a_k_prompt = jnp.stack(st_p["a_k"])
    new_a_v_prompt = jnp.stack(st_p["a_v"])
    new_b_k_prompt = jnp.stack(st_p["b_k"])
    new_b_v_prompt = jnp.stack(st_p["b_v"])
    new_b_kidx_prompt = jnp.stack(st_p["b_kidx"])
    new_c_k_prompt = jnp.stack(st_p["c_k"])
    new_c_v_prompt = jnp.stack(st_p["c_v"])
    new_a_k_sample = jnp.stack(st_s["a_k"])
    new_a_v_sample = jnp.stack(st_s["a_v"])
    new_b_k_sample = jnp.stack(st_s["b_k"])
    new_b_v_sample = jnp.stack(st_s["b_v"])
    new_b_kidx_sample = jnp.stack(st_s["b_kidx"])
    new_c_k_sample = jnp.stack(st_s["c_k"])
    new_c_v_sample = jnp.stack(st_s["c_v"])
    return (y_prompt, y_sample, new_a_k_prompt, new_a_v_prompt, new_b_k_prompt, new_b_v_prompt,
            new_b_kidx_prompt, new_c_k_prompt, new_c_v_prompt, new_a_k_sample, new_a_v_sample,
            new_b_k_sample, new_b_v_sample, new_b_kidx_sample, new_c_k_sample, new_c_v_sample)
```

```python
import functools

import jax
import jax.numpy as jnp
from jax import lax
from jax.experimental import pallas as pl
from jax.experimental.pallas import tpu as pltpu

F32 = jnp.float32
BF16 = jnp.bfloat16

CHUNK = 64
D_HEAD = 64
N_HEADS = 8
W_BR = N_HEADS * D_HEAD
DKV_B = 64
H_IDX = 4
D_IDX = 64
TOPK_MAX = 256
BAND_CHUNKS = 8
BAND_ROWS = BAND_CHUNKS * CHUNK
REL_CLIP = 128
N_BRANCH = 3
LN_EPS = 1e-5
NEG_INF = -1e30
ATT_SCALE = D_HEAD ** -0.5
IDX_HEAD_SCALE = H_IDX ** -0.5

LANE = 128
KEY_BLOCK = 128
SAMPLE_PAD = 128
VMEM_LIMIT = 56 * 1024 * 1024

KEY_LO = -2139095040
KEY_HI = 2139095040


def _cparams(n_axes):
    return pltpu.CompilerParams(
        dimension_semantics=("arbitrary",) * n_axes, vmem_limit_bytes=VMEM_LIMIT)


def _sigmoid(v):
    return 1.0 / (1.0 + jnp.exp(-v))


def _layer_norm(x):
    mu = jnp.mean(x, axis=-1, keepdims=True)
    xc = x - mu
    var = jnp.mean(xc * xc, axis=-1, keepdims=True)
    return xc * lax.rsqrt(var + LN_EPS)


def _dot(a, b):
    return jnp.dot(a, b, preferred_element_type=F32)


def _dot_nt(a, b):
    return lax.dot_general(a, b, (((1,), (1,)), ((), ())), preferred_element_type=F32)


def _mod_kernel(c_ref, w_ref, b_ref, o_ref):
    c = c_ref[...]
    s = c * _sigmoid(c)
    o_ref[0] = jnp.dot(s, w_ref[0], preferred_element_type=F32,
                       precision=lax.Precision.HIGHEST) + b_ref[0]


def _modulation(c_all, w_ada, b_ada):
    depth, d, d3 = w_ada.shape
    n = c_all.shape[0]
    tn = 1024
    return pl.pallas_call(
        _mod_kernel,
        grid=(depth, d3 // tn),
        in_specs=[pl.BlockSpec((n, d), lambda l, j: (0, 0)),
                  pl.BlockSpec((1, d, tn), lambda l, j: (l, 0, j)),
                  pl.BlockSpec((1, 1, tn), lambda l, j: (l, 0, j))],
        out_specs=pl.BlockSpec((1, n, tn), lambda l, j: (l, 0, j)),
        out_shape=jax.ShapeDtypeStruct((depth, n, d3), F32),
        compiler_params=_cparams(2),
        name="modulation",
    )(c_all, w_ada, b_ada.reshape(depth, 1, d3))


BAND_TQ = 256
BAND_W = BAND_ROWS + BAND_TQ
BIAS_ROWS_PER_STEP = 8
REL_PAD = 384


def _band_bias_kernel(tab_ref, o_ref):
    r0 = pl.program_id(1) * BIAS_ROWS_PER_STEP
    tab = tab_ref[0]
    col = lax.broadcasted_iota(jnp.int32, (1, BAND_W), 1)
    sub = lax.broadcasted_iota(jnp.int32, (REL_PAD, BAND_W), 0)
    for rr in range(BIAS_ROWS_PER_STEP):
        r = r0 + rr
        idx = jnp.clip(r - col + BAND_ROWS, -REL_CLIP, REL_CLIP) + REL_CLIP
        onehot = jnp.where(sub == idx, 1.0, 0.0).astype(F32)
        vals = jnp.dot(tab, onehot, preferred_element_type=F32,
                       precision=lax.Precision.HIGHEST)
        kk = col - lax.shift_right_logical(r, 6) * CHUNK
        inside = (kk >= 0) & (kk < BAND_ROWS + CHUNK)
        o_ref[0, rr] = jnp.where(inside, vals, NEG_INF)


def _band_bias(rel_bias):
    depth, nh, nrel = rel_bias.shape
    tab = jnp.pad(rel_bias, ((0, 0), (0, 0), (0, REL_PAD - nrel)))
    out = pl.pallas_call(
        _band_bias_kernel,
        grid=(depth, BAND_TQ // BIAS_ROWS_PER_STEP),
        in_specs=[pl.BlockSpec((1, nh, REL_PAD), lambda l, i: (l, 0, 0))],
        out_specs=pl.BlockSpec((1, BIAS_ROWS_PER_STEP, nh, BAND_W), lambda l, i: (l, i, 0, 0)),
        out_shape=jax.ShapeDtypeStruct((depth, BAND_TQ, nh, BAND_W), F32),
        compiler_params=_cparams(2),
        name="band_bias",
    )(tab)
    return jnp.transpose(out, (0, 2, 1, 3))


N_W1 = 6 * W_BR
N_W2 = 256
N_WT = 840
ROW_BIQ = W_BR
ROW_BV = W_BR + H_IDX * D_IDX
ROW_BIW = ROW_BV + DKV_B


def _in_proj_kernel(x_ref, mod_ref, w1_ref, w2_ref, wt_ref,
                    aq_ref, ak_ref, av_ref, akb_ref, avb_ref, cq_ref, ckb_ref, cvb_ref,
                    ckt_ref, cvt_ref, bk_ref, bv_ref, bik_ref, bkb_ref, bikb_ref,
                    bqt_ref, biqt_ref, bvt_ref, biwt_ref, *, first_tail_step):
    x = x_ref[0]
    d = x.shape[-1]
    tm = x.shape[0]
    mod = mod_ref[0]
    shift = mod[:, :d]
    scale = mod[:, d:2 * d]
    h = (_layer_norm(x) * (1.0 + scale) + shift).astype(BF16)

    def proj(j):
        return _dot(h, w1_ref[:, j * W_BR:(j + 1) * W_BR])

    aq_ref[0] = proj(0).astype(BF16)
    ak = proj(1)
    ak_ref[0] = ak
    akb_ref[0] = ak.astype(BF16)
    av = proj(2)
    av_ref[0] = av
    avb_ref[0] = av.astype(BF16)
    cq_ref[0] = proj(3).astype(BF16)
    ck = proj(4)
    ckb_ref[0] = ck.astype(BF16)
    cv = proj(5)
    cvb_ref[0] = cv.astype(BF16)

    @pl.when(pl.program_id(1) >= first_tail_step)
    def _():
        ckt_ref[0] = ck
        cvt_ref[0] = cv

    small = _dot(h, w2_ref[...])
    bk = small[:, 0:DKV_B]
    bik = small[:, 2 * DKV_B:3 * DKV_B]
    bk_ref[0] = bk
    bv_ref[0] = small[:, DKV_B:2 * DKV_B]
    bik_ref[0] = bik
    bkb_ref[0] = bk.astype(BF16)
    bikb_ref[0] = bik.astype(BF16)

    tr = _dot_nt(wt_ref[...], h)
    bqt_ref[0] = tr[0:ROW_BIQ].astype(BF16)
    biqt_ref[0] = tr[ROW_BIQ:ROW_BV].astype(BF16)
    bvt = tr[ROW_BV:ROW_BIW].astype(BF16)
    for c in range(tm // KEY_BLOCK):
        bvt_ref[0, c] = bvt[:, c * KEY_BLOCK:(c + 1) * KEY_BLOCK]
    biwt_ref[0] = tr[ROW_BIW:N_WT]


def _in_proj(x, mod, w1, w2, wt, tm):
    b, t, d = x.shape
    nt = t // tm
    tail = min(BAND_ROWS, t)
    tail_blocks = tail // tm
    first_tail = nt - tail_blocks
    row = lambda bb, i: (bb, i, 0)
    colmaj = lambda bb, i: (bb, 0, i)
    tail_map = lambda bb, i: (bb, jnp.maximum(i - first_tail, 0), 0)

    def rows(n, dt):
        return pl.BlockSpec((1, tm, n), row), jax.ShapeDtypeStruct((b, t, n), dt)

    outs = [
        rows(W_BR, BF16),
        rows(W_BR, F32), rows(W_BR, F32),
        rows(W_BR, BF16), rows(W_BR, BF16),
        rows(W_BR, BF16), rows(W_BR, BF16), rows(W_BR, BF16),
        (pl.BlockSpec((1, tm, W_BR), tail_map), jax.ShapeDtypeStruct((b, tail, W_BR), F32)),
        (pl.BlockSpec((1, tm, W_BR), tail_map), jax.ShapeDtypeStruct((b, tail, W_BR), F32)),
        rows(DKV_B, F32), rows(DKV_B, F32), rows(D_IDX, F32),
        rows(DKV_B, BF16), rows(D_IDX, BF16),
        (pl.BlockSpec((1, W_BR, tm), colmaj), jax.ShapeDtypeStruct((b, W_BR, t), BF16)),
        (pl.BlockSpec((1, H_IDX * D_IDX, tm), colmaj),
         jax.ShapeDtypeStruct((b, H_IDX * D_IDX, t), BF16)),
        (pl.BlockSpec((1, tm // KEY_BLOCK, DKV_B, KEY_BLOCK), lambda bb, i: (bb, i, 0, 0)),
         jax.ShapeDtypeStruct((b, t // KEY_BLOCK, DKV_B, KEY_BLOCK), BF16)),
        (pl.BlockSpec((1, N_WT - ROW_BIW, tm), colmaj),
         jax.ShapeDtypeStruct((b, N_WT - ROW_BIW, t), F32)),
    ]
    return pl.pallas_call(
        functools.partial(_in_proj_kernel, first_tail_step=first_tail),
        grid=(b, nt),
        in_specs=[pl.BlockSpec((1, tm, d), row),
                  pl.BlockSpec((1, 1, mod.shape[-1]), lambda bb, i: (bb, 0, 0)),
                  pl.BlockSpec((d, N_W1), lambda bb, i: (0, 0)),
                  pl.BlockSpec((d, N_W2), lambda bb, i: (0, 0)),
                  pl.BlockSpec((N_WT, d), lambda bb, i: (0, 0))],
        out_specs=[o[0] for o in outs],
        out_shape=[o[1] for o in outs],
        compiler_params=_cparams(2),
        name="in_proj",
    )(x, mod, w1, w2, wt)


def _stick_kernel(q_ref, kd_ref, vd_ref, km_ref, vm_ref, o_ref, *, n_main):
    tq = q_ref.shape[1]
    tk = KEY_BLOCK
    qb = pl.program_id(1)
    n_blocks = qb if n_main is None else n_main

    jj = lax.broadcasted_iota(jnp.int32, (tk, tk), 0)
    ss = lax.broadcasted_iota(jnp.int32, (tk, tk), 1)
    upper = jnp.where(jj >= ss, 1.0, 0.0).astype(BF16)
    uo = jnp.concatenate([upper, jnp.ones((tk, tk), BF16)], axis=1)
    uu = jnp.concatenate([uo, uo], axis=0)
    qi = lax.broadcasted_iota(jnp.int32, (tq, tk), 0)
    kj = lax.broadcasted_iota(jnp.int32, (tq, tk), 1)
    strictly_earlier = kj < qi

    def block(qh, kblk, vblk, carry, acc, masked):
        z = _dot_nt(qh, kblk)
        sp = jnp.maximum(z, 0.0) + jnp.log1p(jnp.exp(-jnp.abs(z)))
        if masked:
            sp = jnp.where(strictly_earlier, sp, 0.0)
        hi = sp.astype(BF16)
        lo = (sp - hi.astype(F32)).astype(BF16)
        sr = _dot(jnp.concatenate([hi, lo], axis=1), uu)
        a = jnp.exp(z - sr[:, :tk] - carry)
        if masked:
            a = jnp.where(strictly_earlier, a, 0.0)
        acc = acc + _dot(a.astype(BF16), vblk)
        return carry + sr[:, tk:], acc

    outs = []
    for h in range(N_HEADS):
        lanes = slice(h * D_HEAD, (h + 1) * D_HEAD)
        qh = q_ref[0, :, lanes]
        carry, acc = block(qh, kd_ref[0, :, lanes].astype(BF16), vd_ref[0, :, lanes].astype(BF16),
                           jnp.zeros((tq, tk), F32), jnp.zeros((tq, D_HEAD), F32), True)

        def body(i, st, qh=qh, lanes=lanes):
            start = pl.multiple_of((n_blocks - 1 - i) * tk, tk)
            kblk = km_ref[0, pl.ds(start, tk), lanes].astype(BF16)
            vblk = vm_ref[0, pl.ds(start, tk), lanes].astype(BF16)
            return block(qh, kblk, vblk, st[0], st[1], False)

        carry, acc = lax.fori_loop(0, n_blocks, body, (carry, acc))
        outs.append(acc)
    o_ref[0] = jnp.concatenate(outs, axis=1)


def _stick(q, k_diag, v_diag, k_main, v_main, *, n_q_blocks, n_main):
    b = q.shape[0]
    tq = KEY_BLOCK
    tm = k_main.shape[1]
    blk = lambda bb, i: (bb, i, 0)
    whole = lambda bb, i: (bb, 0, 0)
    return pl.pallas_call(
        functools.partial(_stick_kernel, n_main=n_main),
        grid=(b, n_q_blocks),
        in_specs=[pl.BlockSpec((1, tq, W_BR), blk),
                  pl.BlockSpec((1, KEY_BLOCK, W_BR), blk),
                  pl.BlockSpec((1, KEY_BLOCK, W_BR), blk),
                  pl.BlockSpec((1, tm, W_BR), whole),
                  pl.BlockSpec((1, tm, W_BR), whole)],
        out_specs=pl.BlockSpec((1, tq, W_BR), blk),
        out_shape=jax.ShapeDtypeStruct((b, n_q_blocks * tq, W_BR), F32),
        compiler_params=_cparams(2),
        name="stick_breaking",
    )(q, k_diag, v_diag, k_main, v_main)


def _key_to_float(key):
    bits = key ^ (lax.shift_right_arithmetic(key, 31) & jnp.int32(0x7FFFFFFF))
    return lax.bitcast_convert_type(bits, F32)


def _dsa_kernel(biqt_ref, biwt_ref, bqt_ref, ik_ref, k_ref, vt_ref, o_ref, sc_ref,
                *, n_full, q_off, n_valid, n_sel):
    tq = bqt_ref.shape[2]
    tk = KEY_BLOCK
    qb = pl.program_id(1)
    n_full_blocks = qb if n_full is None else n_full
    n_blocks = n_full_blocks + 1
    iw = biwt_ref[0]

    kloc = lax.broadcasted_iota(jnp.int32, (tk, tq), 0)
    qpos = q_off + qb * tq + lax.broadcasted_iota(jnp.int32, (tk, tq), 1)

    def score_block(kb):
        start = pl.multiple_of(kb * tk, tk)
        ikb = ik_ref[0, pl.ds(start, tk), :].astype(BF16)
        sc = jnp.zeros((tk, tq), F32)
        for h in range(H_IDX):
            lg = _dot(ikb, biqt_ref[0, h * D_IDX:(h + 1) * D_IDX, :])
            sc = sc + iw[h:h + 1, :] * jnp.maximum(lg, 0.0)
        return jnp.where(sc == 0.0, 0.0, sc)

    def fill(kb, c):
        sc_ref[kb] = score_block(kb)
        return c

    lax.fori_loop(0, n_full_blocks, fill, 0)
    kpos_last = n_full_blocks * tk + kloc
    admissible = ((lax.shift_right_logical(kpos_last, 6) <= lax.shift_right_logical(qpos, 6))
                  & (kpos_last < n_valid))
    sc_ref[n_full_blocks] = jnp.where(admissible, score_block(n_full_blocks), -jnp.inf)

    def count(pred):
        def body(kb, acc):
            return acc + jnp.where(pred(sc_ref[kb]), 1.0, 0.0)
        acc = lax.fori_loop(0, n_blocks, body, jnp.zeros((tk, tq), F32))
        return jnp.sum(acc, axis=0, keepdims=True)

    k_sel = jnp.float32(n_sel)

    def bisect(_, lohi):
        lo, hi = lohi
        mid = lo + lax.shift_right_logical(hi - lo, 1)
        cand = _key_to_float(mid)
        ge = count(lambda x: x >= cand) >= k_sel
        return jnp.where(ge, mid, lo), jnp.where(ge, hi, mid)

    lo, _ = lax.fori_loop(
        0, 32, bisect,
        (jnp.full((1, tq), KEY_LO, jnp.int32), jnp.full((1, tq), KEY_HI, jnp.int32)))
    thr = _key_to_float(lo)
    need = k_sel - count(lambda x: x > thr)

    srow = lax.broadcasted_iota(jnp.int32, (tk, tk), 0)
    jcol = lax.broadcasted_iota(jnp.int32, (tk, tk), 1)
    before = jnp.where(jcol < srow, 1.0, 0.0).astype(BF16)

    def select(kb, seen):
        x = sc_ref[kb]
        eq = x == thr
        eqf = jnp.where(eq, 1.0, 0.0)
        prefix = _dot(before, eqf.astype(BF16)) + seen
        chosen = (x > thr) | (eq & (prefix < need))
        sc_ref[kb] = jnp.where(chosen, 0.0, NEG_INF)
        return seen + jnp.sum(eqf, axis=0, keepdims=True)

    lax.fori_loop(0, n_blocks, select, jnp.zeros((1, tq), F32))

    q_all = jnp.concatenate(
        [bqt_ref[0, h * D_HEAD:(h + 1) * D_HEAD, :] for h in range(N_HEADS)], axis=1)

    def attend(kb, st):
        m, l, acc = st
        start = pl.multiple_of(kb * tk, tk)
        kblk = k_ref[0, pl.ds(start, tk), :].astype(BF16)
        s = _dot(kblk, q_all)
        dist = jnp.abs(qpos - (kb * tk + kloc)).astype(F32)
        mb = sc_ref[kb]
        bias = jnp.concatenate(
            [mb - (2.0 ** -(h + 1)) * dist for h in range(N_HEADS)], axis=1)
        s = s + bias
        m_new = jnp.maximum(m, jnp.max(s, axis=0, keepdims=True))
        p = jnp.exp(s - m_new)
        alpha = jnp.exp(m - m_new)
        l = alpha * l + jnp.sum(p, axis=0, keepdims=True)
        acc = alpha * acc + _dot(vt_ref[0, kb].astype(BF16), p.astype(BF16))
        return m_new, l, acc

    _, l, acc = lax.fori_loop(
        0, n_blocks, attend,
        (jnp.full((1, N_HEADS * tq), -3e38, F32), jnp.zeros((1, N_HEADS * tq), F32),
         jnp.zeros((DKV_B, N_HEADS * tq), F32)))
    y_t = acc / l
    o_ref[0] = jnp.concatenate(
        [y_t[:, h * tq:(h + 1) * tq].T for h in range(N_HEADS)], axis=1)


def _dsa(biqt, biwt, bqt, ik, k, vt, *, n_q_blocks, n_full, q_off, n_valid, n_sel):
    b = bqt.shape[0]
    tq = KEY_BLOCK
    tk_total = ik.shape[1]
    colmaj = lambda bb, i: (bb, 0, i)
    whole3 = lambda bb, i: (bb, 0, 0)
    return pl.pallas_call(
        functools.partial(_dsa_kernel, n_full=n_full, q_off=q_off, n_valid=n_valid, n_sel=n_sel),
        grid=(b, n_q_blocks),
        in_specs=[pl.BlockSpec((1, H_IDX * D_IDX, tq), colmaj),
                  pl.BlockSpec((1, biwt.shape[1], tq), colmaj),
                  pl.BlockSpec((1, W_BR, tq), colmaj),
                  pl.BlockSpec((1, tk_total, D_IDX), whole3),
                  pl.BlockSpec((1, tk_total, DKV_B), whole3),
                  pl.BlockSpec((1, tk_total // KEY_BLOCK, DKV_B, KEY_BLOCK),
                               lambda bb, i: (bb, 0, 0, 0))],
        out_specs=pl.BlockSpec((1, tq, W_BR), lambda bb, i: (bb, i, 0)),
        out_shape=jax.ShapeDtypeStruct((b, n_q_blocks * tq, W_BR), F32),
        scratch_shapes=[pltpu.VMEM((tk_total // KEY_BLOCK, KEY_BLOCK, tq), F32)],
        compiler_params=_cparams(2),
        name="dsa",
    )(biqt, biwt, bqt, ik, k, vt)


def _band_kernel(*refs, n_win, win_start_blocks, n_valid_win):
    q_ref = refs[0]
    k_refs = refs[1:1 + n_win]
    v_refs = refs[1 + n_win:1 + 2 * n_win]
    bias_ref = refs[1 + 2 * n_win]
    o_ref = refs[2 + 2 * n_win]
    tq = q_ref.shape[1]
    w = n_win * tq
    i = pl.program_id(1)
    col = lax.broadcasted_iota(jnp.int32, (1, w), 1)
    if win_start_blocks is None:
        valid = col < n_valid_win
    else:
        valid = ((i + win_start_blocks) * tq + col >= 0) & (col < n_valid_win)
    kwin = jnp.concatenate([r[0] for r in k_refs], axis=0)
    vwin = jnp.concatenate([r[0] for r in v_refs], axis=0)
    outs = []
    for h in range(N_HEADS):
        lanes = slice(h * D_HEAD, (h + 1) * D_HEAD)
        s = _dot_nt(q_ref[0, :, lanes], kwin[:, lanes]) + bias_ref[h]
        s = jnp.where(valid, s, NEG_INF)
        e = jnp.exp(s - jnp.max(s, axis=-1, keepdims=True))
        den = jnp.sum(e, axis=-1, keepdims=True)
        outs.append(_dot(e.astype(BF16), vwin[:, lanes]) / den)
    o_ref[0] = jnp.concatenate(outs, axis=1)


def _band(q, k, v, bias, *, tq, n_q_blocks, sliding, n_valid_win):
    b = q.shape[0]
    n_win = bias.shape[2] // tq
    assert bias.shape[1] == tq

    def kv_map(wi):
        if sliding:
            return lambda bb, i: (bb, jnp.maximum(i - (n_win - 1) + wi, 0), 0)
        return lambda bb, i: (bb, wi, 0)

    kv_specs = [pl.BlockSpec((1, tq, W_BR), kv_map(wi)) for wi in range(n_win)]
    return pl.pallas_call(
        functools.partial(_band_kernel, n_win=n_win,
                          win_start_blocks=-(n_win - 1) if sliding else None,
                          n_valid_win=n_valid_win),
        grid=(b, n_q_blocks),
        in_specs=([pl.BlockSpec((1, tq, W_BR), lambda bb, i: (bb, i, 0))] + kv_specs + kv_specs
                  + [pl.BlockSpec(bias.shape, lambda bb, i: (0, 0, 0))]),
        out_specs=pl.BlockSpec((1, tq, W_BR), lambda bb, i: (bb, i, 0)),
        out_shape=jax.ShapeDtypeStruct((b, n_q_blocks * tq, W_BR), F32),
        compiler_params=_cparams(2),
        name="band",
    )(q, *([k] * n_win), *([v] * n_win), bias)


def _merge_kernel(x_ref, mod_ref, ya_ref, yb_ref, yc_ref, wz_ref, wg_ref, bg_ref, wbr_ref,
                  wo_ref, lng_ref, lnb_ref, o_ref, *, alpha):
    x = x_ref[0]
    d = x.shape[-1]
    mod = mod_ref[0]
    shift = mod[:, :d]
    scale = mod[:, d:2 * d]
    gate = mod[:, 2 * d:3 * d]
    h = (_layer_norm(x) * (1.0 + scale) + shift).astype(BF16)
    m = jnp.zeros(x.shape, F32)
    for n, y_ref in enumerate((ya_ref, yb_ref, yc_ref)):
        z = _dot(h, wz_ref[:, n * W_BR:(n + 1) * W_BR])
        u = (y_ref[0] * (z * _sigmoid(z))).astype(BF16)
        t = _dot(u, wbr_ref[n])
        g = _sigmoid(_dot(h, wg_ref[n]) + bg_ref[n])
        m = m + g * t
    out = _dot(m.astype(BF16), wo_ref[...])
    r = alpha * x + gate * out
    o_ref[0] = _layer_norm(r) * lng_ref[...] + lnb_ref[...]


def _merge(x, mod, ya, yb, yc, wz, wg, bg, wbr, wo, lng, lnb, *, tm, alpha):
    b, t, d = x.shape
    row = lambda bb, i: (bb, i, 0)
    const2 = lambda bb, i: (0, 0)
    const3 = lambda bb, i: (0, 0, 0)
    return pl.pallas_call(
        functools.partial(_merge_kernel, alpha=alpha),
        grid=(b, t // tm),
        in_specs=[pl.BlockSpec((1, tm, d), row),
                  pl.BlockSpec((1, 1, mod.shape[-1]), lambda bb, i: (bb, 0, 0)),
                  pl.BlockSpec((1, tm, W_BR), row),
                  pl.BlockSpec((1, tm, W_BR), row),
                  pl.BlockSpec((1, tm, W_BR), row),
                  pl.BlockSpec(wz.shape, const2),
                  pl.BlockSpec(wg.shape, const3),
                  pl.BlockSpec(bg.shape, const3),
                  pl.BlockSpec(wbr.shape, const3),
                  pl.BlockSpec(wo.shape, const2),
                  pl.BlockSpec(lng.shape, const2),
                  pl.BlockSpec(lnb.shape, const2)],
        out_specs=pl.BlockSpec((1, tm, d), row),
        out_shape=jax.ShapeDtypeStruct((b, t, d), F32),
        compiler_params=_cparams(2),
        name="merge",
    )(x, mod, ya, yb, yc, wz, wg, bg, wbr, wo, lng, lnb)


def _split_w_in(w):
    sizes = (W_BR, W_BR, W_BR, W_BR, W_BR, DKV_B, DKV_B, W_BR, H_IDX * D_IDX, D_IDX, H_IDX,
             W_BR, W_BR, W_BR, W_BR)
    cols, off = [], 0
    for s in sizes:
        cols.append(w[:, off:off + s])
        off += s
    aq, ak, av, az, bq, bk, bv, bz, biq, bik, biw, cq, ck, cv, cz = cols
    d = w.shape[0]
    w1 = jnp.concatenate([aq * ATT_SCALE, ak, av, cq * ATT_SCALE, ck, cv], axis=1).astype(BF16)
    w2 = jnp.concatenate([bk, bv, bik, jnp.zeros((d, N_W2 - 3 * DKV_B), w.dtype)],
                         axis=1).astype(BF16)
    wt = jnp.concatenate([bq * ATT_SCALE, biq * (D_IDX ** -0.5), bv, biw * IDX_HEAD_SCALE,
                          jnp.zeros((d, N_WT - ROW_BIW - H_IDX), w.dtype)], axis=1).T.astype(BF16)
    wz = jnp.concatenate([az, bz, cz], axis=1).astype(BF16)
    return w1, w2, wt, wz


def _key_blocks_t(v):
    b, t, n = v.shape
    return jnp.transpose(v.reshape(b, t // KEY_BLOCK, KEY_BLOCK, n), (0, 1, 3, 2))


def kernel(x_prompt, x_sample, c_prompt, c_sample, cache_a_k, cache_a_v, cache_b_k, cache_b_v,
           cache_b_kidx, cache_c_k, cache_c_v, w_ada, b_ada, w_in, w_gate, b_gate, w_branch,
           w_out, rel_bias, ln_g, ln_b):
    depth = w_in.shape[0]
    bp, seq, d = x_prompt.shape
    bs, dec_seq, _ = x_sample.shape
    past_len = cache_a_k.shape[2]
    band_rows = cache_c_k.shape[2]
    alpha = (2.0 * depth) ** 0.25
    assert seq % 512 == 0 and dec_seq <= CHUNK and past_len % KEY_BLOCK == 0
    assert band_rows == BAND_ROWS and past_len // CHUNK == (past_len + dec_seq - 1) // CHUNK

    mod_all = _modulation(jnp.concatenate([c_prompt, c_sample], axis=0), w_ada, b_ada)
    bias_all = _band_bias(rel_bias)

    xp = x_prompt
    xs = jnp.pad(x_sample, ((0, 0), (0, SAMPLE_PAD - dec_seq), (0, 0)))
    n_sel_p = min(TOPK_MAX, seq // 4)
    n_sel_s = min(TOPK_MAX, (past_len + dec_seq) // 4)
    tm_p = 512
    st_p = [[] for _ in range(7)]
    st_s = [[] for _ in range(7)]

    for l in range(depth):
        w1, w2, wt, wz = _split_w_in(w_in[l])
        wg = w_gate[l].astype(BF16)
        bg = b_gate[l].reshape(N_BRANCH, 1, d)
        wbr = w_branch[l].astype(BF16)
        wo = w_out[l].astype(BF16)
        lng = ln_g[l].reshape(1, d)
        lnb = ln_b[l].reshape(1, d)
        mod_p = mod_all[l, :bp].reshape(bp, 1, 3 * d)
        mod_s = mod_all[l, bp:].reshape(bs, 1, 3 * d)
        bias_p = bias_all[l]

        (aq, ak, av, akb, avb, cq, ckb, cvb, ckt, cvt, bk, bv, bik, bkb, bikb,
         bqt, biqt, bvt, biwt) = _in_proj(xp, mod_p, w1, w2, wt, tm_p)
        ya = _stick(aq, akb, avb, akb, avb, n_q_blocks=seq // KEY_BLOCK, n_main=None)
        yb = _dsa(biqt, biwt, bqt, bikb, bkb, bvt, n_q_blocks=seq // KEY_BLOCK, n_full=None,
                  q_off=0, n_valid=seq, n_sel=n_sel_p)
        yc = _band(cq, ckb, cvb, bias_p, tq=BAND_TQ, n_q_blocks=seq // BAND_TQ, sliding=True,
                   n_valid_win=BAND_W)
        xp = _merge(xp, mod_p, ya, yb, yc, wz, wg, bg, wbr, wo, lng, lnb, tm=256, alpha=alpha)
        for lst, val in zip(st_p, (ak.reshape(bp, seq, N_HEADS, D_HEAD),
                                   av.reshape(bp, seq, N_HEADS, D_HEAD), bk, bv, bik,
                                   ckt.reshape(bp, -1, N_HEADS, D_HEAD),
                                   cvt.reshape(bp, -1, N_HEADS, D_HEAD))):
            lst.append(val)

        (aq, ak, av, akb, avb, cq, ckb, cvb, ckt, cvt, bk, bv, bik, bkb, bikb,
         bqt, biqt, bvt, biwt) = _in_proj(xs, mod_s, w1, w2, wt, SAMPLE_PAD)
        ya = _stick(aq, akb, avb, cache_a_k[l].reshape(bs, past_len, W_BR),
                    cache_a_v[l].reshape(bs, past_len, W_BR),
                    n_q_blocks=1, n_main=past_len // KEY_BLOCK)
        ik_cat = jnp.concatenate([cache_b_kidx[l].astype(BF16), bikb], axis=1)
        k_cat = jnp.concatenate([cache_b_k[l].astype(BF16), bkb], axis=1)
        vt_cat = jnp.concatenate([_key_blocks_t(cache_b_v[l].astype(BF16)), bvt], axis=1)
        yb = _dsa(biqt, biwt, bqt, ik_cat, k_cat, vt_cat, n_q_blocks=1,
                  n_full=past_len // KEY_BLOCK, q_off=past_len, n_valid=past_len + dec_seq,
                  n_sel=n_sel_s)
        kc_cat = jnp.concatenate(
            [cache_c_k[l].reshape(bs, band_rows, W_BR).astype(BF16), ckb], axis=1)
        vc_cat = jnp.concatenate(
            [cache_c_v[l].reshape(bs, band_rows, W_BR).astype(BF16), cvb], axis=1)
        yc = _band(cq, kc_cat, vc_cat, bias_p[:, :SAMPLE_PAD, :band_rows + SAMPLE_PAD],
                   tq=SAMPLE_PAD, n_q_blocks=1, sliding=False, n_valid_win=band_rows + dec_seq)
        xs = _merge(xs, mod_s, ya, yb, yc, wz, wg, bg, wbr, wo, lng, lnb, tm=SAMPLE_PAD,
                    alpha=alpha)
        new_ck = jnp.concatenate(
            [cache_c_k[l], ckt[:, :dec_seq].reshape(bs, dec_seq, N_HEADS, D_HEAD)],
            axis=1)[:, -band_rows:]
        new_cv = jnp.concatenate(
            [cache_c_v[l], cvt[:, :dec_seq].reshape(bs, dec_seq, N_HEADS, D_HEAD)],
            axis=1)[:, -band_rows:]
        for lst, val in zip(st_s, (ak[:, :dec_seq].reshape(bs, dec_seq, N_HEADS, D_HEAD),
                                   av[:, :dec_seq].reshape(bs, dec_seq, N_HEADS, D_HEAD),
                                   bk[:, :dec_seq], bv[:, :dec_seq], bik[:, :dec_seq],
                                   new_ck, new_cv)):
            lst.append(val)

    return (xp, xs[:, :dec_seq], *[jnp.stack(v) for v in st_p], *[jnp.stack(v) for v in st_s])
```

```python
import functools

import jax
import jax.numpy as jnp
from jax import lax
from jax.experimental import pallas as pl
from jax.experimental.pallas import tpu as pltpu

F32 = jnp.float32
BF16 = jnp.bfloat16

CHUNK = 64
D_HEAD = 64
N_HEADS = 8
W_BR = N_HEADS * D_HEAD
DKV_B = 64
H_IDX = 4
D_IDX = 64
TOPK_MAX = 256
BAND_CHUNKS = 8
BAND_ROWS = BAND_CHUNKS * CHUNK
REL_CLIP = 128
N_BRANCH = 3
LN_EPS = 1e-5
NEG_INF = -1e30
ATT_SCALE = D_HEAD ** -0.5
IDX_HEAD_SCALE = H_IDX ** -0.5

LANE = 128
KEY_BLOCK = 128
SAMPLE_PAD = 128
VMEM_LIMIT = 56 * 1024 * 1024

KEY_LO = -2139095040
KEY_HI = 2139095040


def _cparams(n_axes):
    return pltpu.CompilerParams(
        dimension_semantics=("arbitrary",) * n_axes, vmem_limit_bytes=VMEM_LIMIT)


def _sigmoid(v):
    return 1.0 / (1.0 + jnp.exp(-v))


def _layer_norm(x):
    mu = jnp.mean(x, axis=-1, keepdims=True)
    xc = x - mu
    var = jnp.mean(xc * xc, axis=-1, keepdims=True)
    return xc * lax.rsqrt(var + LN_EPS)


def _dot(a, b):
    return jnp.dot(a, b, preferred_element_type=F32)


def _dot_nt(a, b):
    return lax.dot_general(a, b, (((1,), (1,)), ((), ())), preferred_element_type=F32)


def _mod_kernel(c_ref, w_ref, b_ref, o_ref):
    c = c_ref[...]
    s = c * _sigmoid(c)
    o_ref[0] = jnp.dot(s, w_ref[0], preferred_element_type=F32,
                       precision=lax.Precision.HIGHEST) + b_ref[0]


def _modulation(c_all, w_ada, b_ada):
    depth, d, d3 = w_ada.shape
    n = c_all.shape[0]
    tn = 1024
    return pl.pallas_call(
        _mod_kernel,
        grid=(depth, d3 // tn),
        in_specs=[pl.BlockSpec((n, d), lambda l, j: (0, 0)),
                  pl.BlockSpec((1, d, tn), lambda l, j: (l, 0, j)),
                  pl.BlockSpec((1, 1, tn), lambda l, j: (l, 0, j))],
        out_specs=pl.BlockSpec((1, n, tn), lambda l, j: (l, 0, j)),
        out_shape=jax.ShapeDtypeStruct((depth, n, d3), F32),
        compiler_params=_cparams(2),
        name="modulation",
    )(c_all, w_ada, b_ada.reshape(depth, 1, d3))


BAND_TQ = 256
BAND_W = BAND_ROWS + BAND_TQ
BIAS_ROWS_PER_STEP = 8
REL_PAD = 384


def _band_bias_kernel(tab_ref, o_ref):
    r0 = pl.program_id(1) * BIAS_ROWS_PER_STEP
    tab = tab_ref[0]
    col = lax.broadcasted_iota(jnp.int32, (1, BAND_W), 1)
    sub = lax.broadcasted_iota(jnp.int32, (REL_PAD, BAND_W), 0)
    for rr in range(BIAS_ROWS_PER_STEP):
        r = r0 + rr
        idx = jnp.clip(r - col + BAND_ROWS, -REL_CLIP, REL_CLIP) + REL_CLIP
        onehot = jnp.where(sub == idx, 1.0, 0.0).astype(F32)
        vals = jnp.dot(tab, onehot, preferred_element_type=F32,
                       precision=lax.Precision.HIGHEST)
        kk = col - lax.shift_right_logical(r, 6) * CHUNK
        inside = (kk >= 0) & (kk < BAND_ROWS + CHUNK)
        o_ref[0, rr] = jnp.where(inside, vals, NEG_INF)


def _band_bias(rel_bias):
    depth, nh, nrel = rel_bias.shape
    tab = jnp.pad(rel_bias, ((0, 0), (0, 0), (0, REL_PAD - nrel)))
    out = pl.pallas_call(
        _band_bias_kernel,
        grid=(depth, BAND_TQ // BIAS_ROWS_PER_STEP),
        in_specs=[pl.BlockSpec((1, nh, REL_PAD), lambda l, i: (l, 0, 0))],
        out_specs=pl.BlockSpec((1, BIAS_ROWS_PER_STEP, nh, BAND_W), lambda l, i: (l, i, 0, 0)),
        out_shape=jax.ShapeDtypeStruct((depth, BAND_TQ, nh, BAND_W), F32),
        compiler_params=_cparams(2),
        name="band_bias",
    )(tab)
    return jnp.transpose(out, (0, 2, 1, 3))


N_W1 = 6 * W_BR
N_W2 = 256
N_WT = 840
ROW_BIQ = W_BR
ROW_BV = W_BR + H_IDX * D_IDX
ROW_BIW = ROW_BV + DKV_B


def _in_proj_kernel(x_ref, mod_ref, w1_ref, w2_ref, wt_ref,
                    aq_ref, ak_ref, av_ref, akb_ref, avb_ref, cq_ref, ckb_ref, cvb_ref,
                    ckt_ref, cvt_ref, bk_ref, bv_ref, bik_ref, bkb_ref, bikb_ref,
                    bqt_ref, biqt_ref, bvt_ref, biwt_ref, *, first_tail_step):
    x = x_ref[0]
    d = x.shape[-1]
    tm = x.shape[0]
    mod = mod_ref[0]
    shift = mod[:, :d]
    scale = mod[:, d:2 * d]
    h = (_layer_norm(x) * (1.0 + scale) + shift).astype(BF16)

    def proj(j):
        return _dot(h, w1_ref[:, j * W_BR:(j + 1) * W_BR])

    aq_ref[0] = proj(0).astype(BF16)
    ak = proj(1)
    ak_ref[0] = ak
    akb_ref[0] = ak.astype(BF16)
    av = proj(2)
    av_ref[0] = av
    avb_ref[0] = av.astype(BF16)
    cq_ref[0] = proj(3).astype(BF16)
    ck = proj(4)
    ckb_ref[0] = ck.astype(BF16)
    cv = proj(5)
    cvb_ref[0] = cv.astype(BF16)

    @pl.when(pl.program_id(1) >= first_tail_step)
    def _():
        ckt_ref[0] = ck
        cvt_ref[0] = cv

    small = _dot(h, w2_ref[...])
    bk = small[:, 0:DKV_B]
    bik = small[:, 2 * DKV_B:3 * DKV_B]
    bk_ref[0] = bk
    bv_ref[0] = small[:, DKV_B:2 * DKV_B]
    bik_ref[0] = bik
    bkb_ref[0] = bk.astype(BF16)
    bikb_ref[0] = bik.astype(BF16)

    tr = _dot_nt(wt_ref[...], h)
    bqt_ref[0] = tr[0:ROW_BIQ].astype(BF16)
    biqt_ref[0] = tr[ROW_BIQ:ROW_BV].astype(BF16)
    bvt = tr[ROW_BV:ROW_BIW].astype(BF16)
    for c in range(tm // KEY_BLOCK):
        bvt_ref[0, c] = bvt[:, c * KEY_BLOCK:(c + 1) * KEY_BLOCK]
    biwt_ref[0] = tr[ROW_BIW:N_WT]


def _in_proj(x, mod, w1, w2, wt, tm):
    b, t, d = x.shape
    nt = t // tm
    tail = min(BAND_ROWS, t)
    tail_blocks = tail // tm
    first_tail = nt - tail_blocks
    row = lambda bb, i: (bb, i, 0)
    colmaj = lambda bb, i: (bb, 0, i)
    tail_map = lambda bb, i: (bb, jnp.maximum(i - first_tail, 0), 0)

    def rows(n, dt):
        return pl.BlockSpec((1, tm, n), row), jax.ShapeDtypeStruct((b, t, n), dt)

    outs = [
        rows(W_BR, BF16),
        rows(W_BR, F32), rows(W_BR, F32),
        rows(W_BR, BF16), rows(W_BR, BF16),
        rows(W_BR, BF16), rows(W_BR, BF16), rows(W_BR, BF16),
        (pl.BlockSpec((1, tm, W_BR), tail_map), jax.ShapeDtypeStruct((b, tail, W_BR), F32)),
        (pl.BlockSpec((1, tm, W_BR), tail_map), jax.ShapeDtypeStruct((b, tail, W_BR), F32)),
        rows(DKV_B, F32), rows(DKV_B, F32), rows(D_IDX, F32),
        rows(DKV_B, BF16), rows(D_IDX, BF16),
        (pl.BlockSpec((1, W_BR, tm), colmaj), jax.ShapeDtypeStruct((b, W_BR, t), BF16)),
        (pl.BlockSpec((1, H_IDX * D_IDX, tm), colmaj),
         jax.ShapeDtypeStruct((b, H_IDX * D_IDX, t), BF16)),
        (pl.BlockSpec((1, tm // KEY_BLOCK, DKV_B, KEY_BLOCK), lambda bb, i: (bb, i, 0, 0)),
         jax.ShapeDtypeStruct((b, t // KEY_BLOCK, DKV_B, KEY_BLOCK), BF16)),
        (pl.BlockSpec((1, N_WT - ROW_BIW, tm), colmaj),
         jax.ShapeDtypeStruct((b, N_WT - ROW_BIW, t), F32)),
    ]
    return pl.pallas_call(
        functools.partial(_in_proj_kernel, first_tail_step=first_tail),
        grid=(b, nt),
        in_specs=[pl.BlockSpec((1, tm, d), row),
                  pl.BlockSpec((1, 1, mod.shape[-1]), lambda bb, i: (bb, 0, 0)),
                  pl.BlockSpec((d, N_W1), lambda bb, i: (0, 0)),
                  pl.BlockSpec((d, N_W2), lambda bb, i: (0, 0)),
                  pl.BlockSpec((N_WT, d), lambda bb, i: (0, 0))],
        out_specs=[o[0] for o in outs],
        out_shape=[o[1] for o in outs],
        compiler_params=_cparams(2),
        name="in_proj",
    )(x, mod, w1, w2, wt)


LOG2E = 1.4426950408889634
STICK_TQ = 256


def _stick_kernel(q_ref, kd_ref, vd_ref, km_ref, vm_ref, o_ref, carry_ref, acc_ref, *, n_main):
    tq = q_ref.shape[1]
    tk = KEY_BLOCK
    n_diag = tq // tk
    qb = pl.program_id(1)
    n_blocks = qb * n_diag if n_main is None else n_main

    jj = lax.broadcasted_iota(jnp.int32, (tk, tk), 0)
    ss = lax.broadcasted_iota(jnp.int32, (tk, tk), 1)
    upper = jnp.where(jj >= ss, 1.0, 0.0).astype(BF16)
    uo = jnp.concatenate([upper, jnp.ones((tk, tk), BF16)], axis=1)
    uu = jnp.concatenate([uo, uo], axis=0)
    qi = lax.broadcasted_iota(jnp.int32, (tq, tk), 0)
    kj = lax.broadcasted_iota(jnp.int32, (tq, tk), 1)

    carry_ref[...] = jnp.zeros(carry_ref.shape, F32)
    acc_ref[...] = jnp.zeros(acc_ref.shape, F32)

    def step(kblk, vblk, visible):
        heads = range(N_HEADS)
        lanes = [slice(h * D_HEAD, (h + 1) * D_HEAD) for h in heads]
        zs = [_dot_nt(q_ref[0, :, lanes[h]], kblk[:, lanes[h]]) * LOG2E for h in heads]
        splits = []
        for h in heads:
            sp = jnp.maximum(zs[h], 0.0) + jnp.log2(1.0 + jnp.exp2(-jnp.abs(zs[h])))
            if visible is not None:
                sp = jnp.where(visible, sp, 0.0)
            hi = sp.astype(BF16)
            lo = (sp - hi.astype(F32)).astype(BF16)
            splits.append(jnp.concatenate([hi, lo], axis=1))
        srs = [_dot(splits[h], uu) for h in heads]
        weights = []
        for h in heads:
            a = jnp.exp2(zs[h] - srs[h][:, :tk] - carry_ref[h])
            if visible is not None:
                a = jnp.where(visible, a, 0.0)
            weights.append(a.astype(BF16))
            carry_ref[h] += srs[h][:, tk:]
        for h in heads:
            acc_ref[h] += _dot(weights[h], vblk[:, lanes[h]])

    for dd in reversed(range(n_diag)):
        rows = slice(dd * tk, (dd + 1) * tk)
        step(kd_ref[0, rows, :].astype(BF16), vd_ref[0, rows, :].astype(BF16), dd * tk + kj < qi)

    def body(i, c):
        start = pl.multiple_of((n_blocks - 1 - i) * tk, tk)
        step(km_ref[0, pl.ds(start, tk), :].astype(BF16),
             vm_ref[0, pl.ds(start, tk), :].astype(BF16), None)
        return c

    lax.fori_loop(0, n_blocks, body, 0)
    o_ref[0] = jnp.concatenate([acc_ref[h] for h in range(N_HEADS)], axis=1)


def _stick(q, k_diag, v_diag, k_main, v_main, *, tq, n_q_blocks, n_main):
    b = q.shape[0]
    tm = k_main.shape[1]
    blk = lambda bb, i: (bb, i, 0)
    whole = lambda bb, i: (bb, 0, 0)
    return pl.pallas_call(
        functools.partial(_stick_kernel, n_main=n_main),
        grid=(b, n_q_blocks),
        in_specs=[pl.BlockSpec((1, tq, W_BR), blk),
                  pl.BlockSpec((1, tq, W_BR), blk),
                  pl.BlockSpec((1, tq, W_BR), blk),
                  pl.BlockSpec((1, tm, W_BR), whole),
                  pl.BlockSpec((1, tm, W_BR), whole)],
        out_specs=pl.BlockSpec((1, tq, W_BR), blk),
        out_shape=jax.ShapeDtypeStruct((b, n_q_blocks * tq, W_BR), F32),
        scratch_shapes=[pltpu.VMEM((N_HEADS, tq, KEY_BLOCK), F32),
                        pltpu.VMEM((N_HEADS, tq, D_HEAD), F32)],
        compiler_params=_cparams(2),
        name="stick_breaking",
    )(q, k_diag, v_diag, k_main, v_main)


def _key_to_float(key):
    bits = key ^ (lax.shift_right_arithmetic(key, 31) & jnp.int32(0x7FFFFFFF))
    return lax.bitcast_convert_type(bits, F32)


DSA_TK = 2 * KEY_BLOCK
ALIBI_ROWS = 64
BISECT_GROUP = 4


def _dsa_kernel(biqt_ref, biwt_ref, bqt_ref, ik_ref, k_ref, vt_ref, o_ref, sc_ref,
                *, n_full, q_off, n_valid, n_sel):
    tq = bqt_ref.shape[2]
    tk = DSA_TK
    sub = tk // KEY_BLOCK
    qb = pl.program_id(1)
    n_full_blocks = (qb * tq) // tk if n_full is None else n_full
    n_blocks = n_full_blocks + 1
    iw = biwt_ref[0]
    q0 = q_off + qb * tq

    kloc = lax.broadcasted_iota(jnp.int32, (tk, tq), 0)
    qpos = q0 + lax.broadcasted_iota(jnp.int32, (tk, tq), 1)

    def score_block(kb):
        start = pl.multiple_of(kb * tk, tk)
        ikb = ik_ref[0, pl.ds(start, tk), :].astype(BF16)
        sc = jnp.zeros((tk, tq), F32)
        for h in range(H_IDX):
            lg = _dot(ikb, biqt_ref[0, h * D_IDX:(h + 1) * D_IDX, :])
            sc = sc + iw[h:h + 1, :] * jnp.maximum(lg, 0.0)
        return jnp.where(sc == 0.0, 0.0, sc)

    def fill(kb, c):
        sc_ref[kb] = score_block(kb)
        return c

    lax.fori_loop(0, n_full_blocks, fill, 0)
    kpos_last = n_full_blocks * tk + kloc
    admissible = ((lax.shift_right_logical(kpos_last, 6) <= lax.shift_right_logical(qpos, 6))
                  & (kpos_last < n_valid))
    sc_ref[n_full_blocks] = jnp.where(admissible, score_block(n_full_blocks), -jnp.inf)

    def count(pred):
        def body(kb, acc):
            for j in range(sub):
                x = sc_ref[kb, j * KEY_BLOCK:(j + 1) * KEY_BLOCK, :]
                acc = acc + jnp.where(pred(x), 1.0, 0.0)
            return acc
        acc = lax.fori_loop(0, n_blocks, body, jnp.zeros((KEY_BLOCK, tq), F32))
        return jnp.sum(acc, axis=0, keepdims=True)

    k_sel = jnp.float32(n_sel)

    n_ge0 = count(lambda x: x >= 0.0)
    n_gt0 = count(lambda x: x > 0.0)
    zero_thr = (n_ge0 >= k_sel) & (n_gt0 < k_sel)
    positive = n_gt0 >= k_sel
    lo0 = jnp.where(zero_thr, 0, jnp.where(positive, 1, KEY_LO)).astype(jnp.int32)
    hi0 = jnp.where(zero_thr, 1, jnp.where(positive, KEY_HI, 0)).astype(jnp.int32)

    def unresolved(lo, hi):
        return (hi - lo) != 1

    def bisect_cond(st):
        it, lo, hi = st
        return ((it < 32 // BISECT_GROUP)
                & (jnp.max(jnp.where(unresolved(lo, hi), 1.0, 0.0)) > 0.5))

    def bisect(st):
        it, lo, hi = st
        for _ in range(BISECT_GROUP):
            active = unresolved(lo, hi)
            mid = lo + lax.shift_right_logical(hi - lo, 1)
            n_ge = count(lambda x, mid=mid: x >= _key_to_float(mid))
            ge = n_ge >= k_sel
            exact = n_ge == k_sel
            lo, hi = (jnp.where(active & ge, mid, lo),
                      jnp.where(active, jnp.where(exact, mid + 1, jnp.where(ge, hi, mid)), hi))
        return it + 1, lo, hi

    _, lo, _ = lax.while_loop(bisect_cond, bisect, (jnp.int32(0), lo0, hi0))
    thr = _key_to_float(lo)
    need = k_sel - count(lambda x: x > thr)

    srow = lax.broadcasted_iota(jnp.int32, (tk, tk), 0)
    jcol = lax.broadcasted_iota(jnp.int32, (tk, tk), 1)
    before = jnp.where(jcol < srow, 1.0, 0.0).astype(BF16)

    def select(kb, seen):
        x = sc_ref[kb]
        eq = x == thr
        eqf = jnp.where(eq, 1.0, 0.0)
        prefix = _dot(before, eqf.astype(BF16)) + seen
        chosen = (x > thr) | (eq & (prefix < need))
        sc_ref[kb] = jnp.where(chosen, 0.0, NEG_INF)
        return seen + jnp.sum(eqf, axis=0, keepdims=True)

    lax.fori_loop(0, n_blocks, select, jnp.zeros((1, tq), F32))

    q_all = jnp.concatenate(
        [bqt_ref[0, h * D_HEAD:(h + 1) * D_HEAD, :] for h in range(N_HEADS)], axis=1)
    slope = jnp.concatenate(
        [jnp.full((1, tq), 2.0 ** -(h + 1), F32) for h in range(N_HEADS)], axis=1)
    qi_all = jnp.concatenate(
        [lax.broadcasted_iota(jnp.int32, (1, tq), 1)] * N_HEADS, axis=1).astype(F32)
    arow = lax.broadcasted_iota(jnp.int32, (ALIBI_ROWS, N_HEADS * tq), 0)
    q_extra = jnp.where(arow == 0, -slope * qi_all, jnp.where(arow <= 2, slope, 0.0))
    q_aug = jnp.concatenate([q_all, q_extra.astype(BF16)], axis=0)
    acol = lax.broadcasted_iota(jnp.int32, (tk, ALIBI_ROWS), 1)
    kj_f = lax.broadcasted_iota(jnp.int32, (tk, ALIBI_ROWS), 0).astype(F32)

    def softmax_step(st, s_heads, vblk_t):
        ms, ls, accs = st
        m_new = [jnp.maximum(ms[h], jnp.max(s_heads[h], axis=0, keepdims=True))
                 for h in range(N_HEADS)]
        ps = [jnp.exp(s_heads[h] - m_new[h]) for h in range(N_HEADS)]
        new_l, new_acc = [], []
        for h in range(N_HEADS):
            alpha = jnp.exp(ms[h] - m_new[h])
            new_l.append(alpha * ls[h] + jnp.sum(ps[h], axis=0, keepdims=True))
            new_acc.append(alpha * accs[h] + _dot(vblk_t, ps[h].astype(BF16)))
        return tuple(m_new), tuple(new_l), tuple(new_acc)

    def load_kv(kb):
        start = pl.multiple_of(kb * tk, tk)
        kblk = k_ref[0, pl.ds(start, tk), :].astype(BF16)
        vblk_t = jnp.concatenate(
            [vt_ref[0, kb * sub + j].astype(BF16) for j in range(sub)], axis=1)
        return kblk, vblk_t

    def attend_full(kb, st):
        kblk, vblk_t = load_kv(kb)
        neg_delta = (kb * tk - q0).astype(F32)
        k_extra = jnp.where(acol == 0, 1.0,
                            jnp.where(acol == 1, kj_f, jnp.where(acol == 2, neg_delta, 0.0)))
        k_aug = jnp.concatenate([kblk, k_extra.astype(BF16)], axis=1)
        mb = sc_ref[kb]
        s_heads = [_dot(k_aug, q_aug[:, h * tq:(h + 1) * tq]) + mb for h in range(N_HEADS)]
        return softmax_step(st, s_heads, vblk_t)

    st = lax.fori_loop(
        0, n_full_blocks, attend_full,
        (tuple(jnp.full((1, tq), -3e38, F32) for _ in range(N_HEADS)),
         tuple(jnp.zeros((1, tq), F32) for _ in range(N_HEADS)),
         tuple(jnp.zeros((DKV_B, tq), F32) for _ in range(N_HEADS))))

    kblk, vblk_t = load_kv(n_full_blocks)
    dist = jnp.abs(qpos - kpos_last).astype(F32)
    mb = sc_ref[n_full_blocks]
    s_heads = [_dot(kblk, q_all[:, h * tq:(h + 1) * tq]) + (mb - (2.0 ** -(h + 1)) * dist)
               for h in range(N_HEADS)]
    _, ls, accs = softmax_step(st, s_heads, vblk_t)

    o_ref[0] = jnp.concatenate([(accs[h] / ls[h]).T for h in range(N_HEADS)], axis=1)


def _dsa(biqt, biwt, bqt, ik, k, vt, *, n_q_blocks, n_full, q_off, n_valid, n_sel):
    b = bqt.shape[0]
    tq = KEY_BLOCK
    tk_total = ik.shape[1]
    colmaj = lambda bb, i: (bb, 0, i)
    whole3 = lambda bb, i: (bb, 0, 0)
    return pl.pallas_call(
        functools.partial(_dsa_kernel, n_full=n_full, q_off=q_off, n_valid=n_valid, n_sel=n_sel),
        grid=(b, n_q_blocks),
        in_specs=[pl.BlockSpec((1, H_IDX * D_IDX, tq), colmaj),
                  pl.BlockSpec((1, biwt.shape[1], tq), colmaj),
                  pl.BlockSpec((1, W_BR, tq), colmaj),
                  pl.BlockSpec((1, tk_total, D_IDX), whole3),
                  pl.BlockSpec((1, tk_total, DKV_B), whole3),
                  pl.BlockSpec((1, tk_total // KEY_BLOCK, DKV_B, KEY_BLOCK),
                               lambda bb, i: (bb, 0, 0, 0))],
        out_specs=pl.BlockSpec((1, tq, W_BR), lambda bb, i: (bb, i, 0)),
        out_shape=jax.ShapeDtypeStruct((b, n_q_blocks * tq, W_BR), F32),
        scratch_shapes=[pltpu.VMEM((tk_total // DSA_TK, DSA_TK, tq), F32)],
        compiler_params=_cparams(2),
        name="dsa",
    )(biqt, biwt, bqt, ik, k, vt)


def _band_kernel(*refs, n_win, win_start_blocks, n_valid_win):
    q_ref = refs[0]
    k_refs = refs[1:1 + n_win]
    v_refs = refs[1 + n_win:1 + 2 * n_win]
    bias_ref = refs[1 + 2 * n_win]
    o_ref = refs[2 + 2 * n_win]
    tq = q_ref.shape[1]
    w = n_win * tq
    i = pl.program_id(1)
    col = lax.broadcasted_iota(jnp.int32, (1, w), 1)
    if win_start_blocks is None:
        valid = col < n_valid_win
    else:
        valid = ((i + win_start_blocks) * tq + col >= 0) & (col < n_valid_win)
    kwin = jnp.concatenate([r[0] for r in k_refs], axis=0)
    vwin = jnp.concatenate([r[0] for r in v_refs], axis=0)
    outs = []
    for h in range(N_HEADS):
        lanes = slice(h * D_HEAD, (h + 1) * D_HEAD)
        s = _dot_nt(q_ref[0, :, lanes], kwin[:, lanes]) + bias_ref[h]
        s = jnp.where(valid, s, NEG_INF)
        e = jnp.exp(s - jnp.max(s, axis=-1, keepdims=True))
        den = jnp.sum(e, axis=-1, keepdims=True)
        outs.append(_dot(e.astype(BF16), vwin[:, lanes]) / den)
    o_ref[0] = jnp.concatenate(outs, axis=1)


def _band(q, k, v, bias, *, tq, n_q_blocks, sliding, n_valid_win):
    b = q.shape[0]
    n_win = bias.shape[2] // tq
    assert bias.shape[1] == tq

    def kv_map(wi):
        if sliding:
            return lambda bb, i: (bb, jnp.maximum(i - (n_win - 1) + wi, 0), 0)
        return lambda bb, i: (bb, wi, 0)

    kv_specs = [pl.BlockSpec((1, tq, W_BR), kv_map(wi)) for wi in range(n_win)]
    return pl.pallas_call(
        functools.partial(_band_kernel, n_win=n_win,
                          win_start_blocks=-(n_win - 1) if sliding else None,
                          n_valid_win=n_valid_win),
        grid=(b, n_q_blocks),
        in_specs=([pl.BlockSpec((1, tq, W_BR), lambda bb, i: (bb, i, 0))] + kv_specs + kv_specs
                  + [pl.BlockSpec(bias.shape, lambda bb, i: (0, 0, 0))]),
        out_specs=pl.BlockSpec((1, tq, W_BR), lambda bb, i: (bb, i, 0)),
        out_shape=jax.ShapeDtypeStruct((b, n_q_blocks * tq, W_BR), F32),
        compiler_params=_cparams(2),
        name="band",
    )(q, *([k] * n_win), *([v] * n_win), bias)


def _merge_kernel(x_ref, mod_ref, ya_ref, yb_ref, yc_ref, wz_ref, wg_ref, bg_ref, wbr_ref,
                  wo_ref, lng_ref, lnb_ref, o_ref, *, alpha):
    x = x_ref[0]
    d = x.shape[-1]
    mod = mod_ref[0]
    shift = mod[:, :d]
    scale = mod[:, d:2 * d]
    gate = mod[:, 2 * d:3 * d]
    h = (_layer_norm(x) * (1.0 + scale) + shift).astype(BF16)
    m = jnp.zeros(x.shape, F32)
    for n, y_ref in enumerate((ya_ref, yb_ref, yc_ref)):
        z = _dot(h, wz_ref[:, n * W_BR:(n + 1) * W_BR])
        u = (y_ref[0] * (z * _sigmoid(z))).astype(BF16)
        t = _dot(u, wbr_ref[n])
        g = _sigmoid(_dot(h, wg_ref[n]) + bg_ref[n])
        m = m + g * t
    out = _dot(m.astype(BF16), wo_ref[...])
    r = alpha * x + gate * out
    o_ref[0] = _layer_norm(r) * lng_ref[...] + lnb_ref[...]


def _merge(x, mod, ya, yb, yc, wz, wg, bg, wbr, wo, lng, lnb, *, tm, alpha):
    b, t, d = x.shape
    row = lambda bb, i: (bb, i, 0)
    const2 = lambda bb, i: (0, 0)
    const3 = lambda bb, i: (0, 0, 0)
    return pl.pallas_call(
        functools.partial(_merge_kernel, alpha=alpha),
        grid=(b, t // tm),
        in_specs=[pl.BlockSpec((1, tm, d), row),
                  pl.BlockSpec((1, 1, mod.shape[-1]), lambda bb, i: (bb, 0, 0)),
                  pl.BlockSpec((1, tm, W_BR), row),
                  pl.BlockSpec((1, tm, W_BR), row),
                  pl.BlockSpec((1, tm, W_BR), row),
                  pl.BlockSpec(wz.shape, const2),
                  pl.BlockSpec(wg.shape, const3),
                  pl.BlockSpec(bg.shape, const3),
                  pl.BlockSpec(wbr.shape, const3),
                  pl.BlockSpec(wo.shape, const2),
                  pl.BlockSpec(lng.shape, const2),
                  pl.BlockSpec(lnb.shape, const2)],
        out_specs=pl.BlockSpec((1, tm, d), row),
        out_shape=jax.ShapeDtypeStruct((b, t, d), F32),
        compiler_params=_cparams(2),
        name="merge",
    )(x, mod, ya, yb, yc, wz, wg, bg, wbr, wo, lng, lnb)


def _split_w_in(w):
    sizes = (W_BR, W_BR, W_BR, W_BR, W_BR, DKV_B, DKV_B, W_BR, H_IDX * D_IDX, D_IDX, H_IDX,
             W_BR, W_BR, W_BR, W_BR)
    cols, off = [], 0
    for s in sizes:
        cols.append(w[:, off:off + s])
        off += s
    aq, ak, av, az, bq, bk, bv, bz, biq, bik, biw, cq, ck, cv, cz = cols
    d = w.shape[0]
    w1 = jnp.concatenate([aq * ATT_SCALE, ak, av, cq * ATT_SCALE, ck, cv], axis=1).astype(BF16)
    w2 = jnp.concatenate([bk, bv, bik, jnp.zeros((d, N_W2 - 3 * DKV_B), w.dtype)],
                         axis=1).astype(BF16)
    wt = jnp.concatenate([bq * ATT_SCALE, biq * (D_IDX ** -0.5), bv, biw * IDX_HEAD_SCALE,
                          jnp.zeros((d, N_WT - ROW_BIW - H_IDX), w.dtype)], axis=1).T.astype(BF16)
    wz = jnp.concatenate([az, bz, cz], axis=1).astype(BF16)
    return w1, w2, wt, wz


def _key_blocks_t(v):
    b, t, n = v.shape
    return jnp.transpose(v.reshape(b, t // KEY_BLOCK, KEY_BLOCK, n), (0, 1, 3, 2))


def kernel(x_prompt, x_sample, c_prompt, c_sample, cache_a_k, cache_a_v, cache_b_k, cache_b_v,
           cache_b_kidx, cache_c_k, cache_c_v, w_ada, b_ada, w_in, w_gate, b_gate, w_branch,
           w_out, rel_bias, ln_g, ln_b):
    depth = w_in.shape[0]
    bp, seq, d = x_prompt.shape
    bs, dec_seq, _ = x_sample.shape
    past_len = cache_a_k.shape[2]
    band_rows = cache_c_k.shape[2]
    alpha = (2.0 * depth) ** 0.25
    assert seq % 512 == 0 and dec_seq <= CHUNK and past_len % DSA_TK == 0
    assert band_rows == BAND_ROWS and past_len // CHUNK == (past_len + dec_seq - 1) // CHUNK

    mod_all = _modulation(jnp.concatenate([c_prompt, c_sample], axis=0), w_ada, b_ada)
    bias_all = _band_bias(rel_bias)

    xp = x_prompt
    xs = jnp.pad(x_sample, ((0, 0), (0, SAMPLE_PAD - dec_seq), (0, 0)))
    n_sel_p = min(TOPK_MAX, seq // 4)
    n_sel_s = min(TOPK_MAX, (past_len + dec_seq) // 4)
    tm_p = 512
    st_p = [[] for _ in range(7)]
    st_s = [[] for _ in range(7)]

    for l in range(depth):
        w1, w2, wt, wz = _split_w_in(w_in[l])
        wg = w_gate[l].astype(BF16)
        bg = b_gate[l].reshape(N_BRANCH, 1, d)
        wbr = w_branch[l].astype(BF16)
        wo = w_out[l].astype(BF16)
        lng = ln_g[l].reshape(1, d)
        lnb = ln_b[l].reshape(1, d)
        mod_p = mod_all[l, :bp].reshape(bp, 1, 3 * d)
        mod_s = mod_all[l, bp:].reshape(bs, 1, 3 * d)
        bias_p = bias_all[l]

        (aq, ak, av, akb, avb, cq, ckb, cvb, ckt, cvt, bk, bv, bik, bkb, bikb,
         bqt, biqt, bvt, biwt) = _in_proj(xp, mod_p, w1, w2, wt, tm_p)
        ya = _stick(aq, akb, avb, akb, avb, tq=STICK_TQ, n_q_blocks=seq // STICK_TQ, n_main=None)
        yb = _dsa(biqt, biwt, bqt, bikb, bkb, bvt, n_q_blocks=seq // KEY_BLOCK, n_full=None,
                  q_off=0, n_valid=seq, n_sel=n_sel_p)
        yc = _band(cq, ckb, cvb, bias_p, tq=BAND_TQ, n_q_blocks=seq // BAND_TQ, sliding=True,
                   n_valid_win=BAND_W)
        xp = _merge(xp, mod_p, ya, yb, yc, wz, wg, bg, wbr, wo, lng, lnb, tm=256, alpha=alpha)
        for lst, val in zip(st_p, (ak.reshape(bp, seq, N_HEADS, D_HEAD),
                                   av.reshape(bp, seq, N_HEADS, D_HEAD), bk, bv, bik,
                                   ckt.reshape(bp, -1, N_HEADS, D_HEAD),
                                   cvt.reshape(bp, -1, N_HEADS, D_HEAD))):
            lst.append(val)

        (aq, ak, av, akb, avb, cq, ckb, cvb, ckt, cvt, bk, bv, bik, bkb, bikb,
         bqt, biqt, bvt, biwt) = _in_proj(xs, mod_s, w1, w2, wt, SAMPLE_PAD)
        ya = _stick(aq, akb, avb, cache_a_k[l].reshape(bs, past_len, W_BR),
                    cache_a_v[l].reshape(bs, past_len, W_BR),
                    tq=SAMPLE_PAD, n_q_blocks=1, n_main=past_len // KEY_BLOCK)
        key_pad = jnp.zeros((bs, DSA_TK - SAMPLE_PAD, DKV_B), BF16)
        ik_cat = jnp.concatenate([cache_b_kidx[l].astype(BF16), bikb, key_pad], axis=1)
        k_cat = jnp.concatenate([cache_b_k[l].astype(BF16), bkb, key_pad], axis=1)
        vt_cat = jnp.concatenate([_key_blocks_t(cache_b_v[l].astype(BF16)), bvt,
                                  _key_blocks_t(key_pad)], axis=1)
        yb = _dsa(biqt, biwt, bqt, ik_cat, k_cat, vt_cat, n_q_blocks=1,
                  n_full=past_len // DSA_TK, q_off=past_len, n_valid=past_len + dec_seq,
                  n_sel=n_sel_s)
        kc_cat = jnp.concatenate(
            [cache_c_k[l].reshape(bs, band_rows, W_BR).astype(BF16), ckb], axis=1)
        vc_cat = jnp.concatenate(
            [cache_c_v[l].reshape(bs, band_rows, W_BR).astype(BF16), cvb], axis=1)
        yc = _band(cq, kc_cat, vc_cat, bias_p[:, :SAMPLE_PAD, :band_rows + SAMPLE_PAD],
                   tq=SAMPLE_PAD, n_q_blocks=1, sliding=False, n_valid_win=band_rows + dec_seq)
        xs = _merge(xs, mod_s, ya, yb, yc, wz, wg, bg, wbr, wo, lng, lnb, tm=SAMPLE_PAD,
                    alpha=alpha)
        new_ck = jnp.concatenate(
            [cache_c_k[l], ckt[:, :dec_seq].reshape(bs, dec_seq, N_HEADS, D_HEAD)],
            axis=1)[:, -band_rows:]
        new_cv = jnp.concatenate(
            [cache_c_v[l], cvt[:, :dec_seq].reshape(bs, dec_seq, N_HEADS, D_HEAD)],
            axis=1)[:, -band_rows:]
        for lst, val in zip(st_s, (ak[:, :dec_seq].reshape(bs, dec_seq, N_HEADS, D_HEAD),
                                   av[:, :dec_seq].reshape(bs, dec_seq, N_HEADS, D_HEAD),
                                   bk[:, :dec_seq], bv[:, :dec_seq], bik[:, :dec_seq],
                                   new_ck, new_cv)):
            lst.append(val)

    return (xp, xs[:, :dec_seq], *[jnp.stack(v) for v in st_p], *[jnp.stack(v) for v in st_s])
```

```python
import functools

import jax
import jax.numpy as jnp
from jax import lax
from jax.experimental import pallas as pl
from jax.experimental.pallas import tpu as pltpu

F32 = jnp.float32
BF16 = jnp.bfloat16

CHUNK = 64
D_HEAD = 64
N_HEADS = 8
W_BR = N_HEADS * D_HEAD
DKV_B = 64
H_IDX = 4
D_IDX = 64
TOPK_MAX = 256
BAND_CHUNKS = 8
BAND_ROWS = BAND_CHUNKS * CHUNK
REL_CLIP = 128
N_BRANCH = 3
LN_EPS = 1e-5
NEG_INF = -1e30
ATT_SCALE = D_HEAD ** -0.5
IDX_HEAD_SCALE = H_IDX ** -0.5

LANE = 128
KEY_BLOCK = 128
SAMPLE_PAD = 128
VMEM_LIMIT = 56 * 1024 * 1024

KEY_LO = -2139095040
KEY_HI = 2139095040


def _cparams(n_axes):
    return pltpu.CompilerParams(
        dimension_semantics=("arbitrary",) * n_axes, vmem_limit_bytes=VMEM_LIMIT)


def _sigmoid(v):
    return 1.0 / (1.0 + jnp.exp(-v))


def _layer_norm(x):
    mu = jnp.mean(x, axis=-1, keepdims=True)
    xc = x - mu
    var = jnp.mean(xc * xc, axis=-1, keepdims=True)
    return xc * lax.rsqrt(var + LN_EPS)


def _dot(a, b):
    return jnp.dot(a, b, preferred_element_type=F32)


def _dot_nt(a, b):
    return lax.dot_general(a, b, (((1,), (1,)), ((), ())), preferred_element_type=F32)


def _mod_kernel(c_ref, w_ref, b_ref, o_ref):
    c = c_ref[...]
    s = c * _sigmoid(c)
    o_ref[0] = jnp.dot(s, w_ref[0], preferred_element_type=F32,
                       precision=lax.Precision.HIGHEST) + b_ref[0]


def _modulation(c_all, w_ada, b_ada):
    depth, d, d3 = w_ada.shape
    n = c_all.shape[0]
    tn = 1024
    return pl.pallas_call(
        _mod_kernel,
        grid=(depth, d3 // tn),
        in_specs=[pl.BlockSpec((n, d), lambda l, j: (0, 0)),
                  pl.BlockSpec((1, d, tn), lambda l, j: (l, 0, j)),
                  pl.BlockSpec((1, 1, tn), lambda l, j: (l, 0, j))],
        out_specs=pl.BlockSpec((1, n, tn), lambda l, j: (l, 0, j)),
        out_shape=jax.ShapeDtypeStruct((depth, n, d3), F32),
        compiler_params=_cparams(2),
        name="modulation",
    )(c_all, w_ada, b_ada.reshape(depth, 1, d3))


BAND_TQ = 256
BAND_W = BAND_ROWS + BAND_TQ
BIAS_ROWS_PER_STEP = 8
REL_PAD = 384


BAND_KEYS = BAND_ROWS + CHUNK
BAND_KEYS_PAD = 640
BAND_Q_CHUNKS = BAND_TQ // CHUNK


def _band_bias_kernel(tab_ref, o_ref):
    q0 = pl.program_id(1) * BIAS_ROWS_PER_STEP
    tab = tab_ref[0]
    col = lax.broadcasted_iota(jnp.int32, (1, BAND_KEYS_PAD), 1)
    sub = lax.broadcasted_iota(jnp.int32, (REL_PAD, BAND_KEYS_PAD), 0)
    nh = tab.shape[0]
    for rr in range(BIAS_ROWS_PER_STEP):
        idx = jnp.clip(q0 + rr - col + BAND_ROWS, -REL_CLIP, REL_CLIP) + REL_CLIP
        onehot = jnp.where(sub == idx, 1.0, 0.0).astype(F32)
        vals = jnp.dot(tab, onehot, preferred_element_type=F32,
                       precision=lax.Precision.HIGHEST)[:, :BAND_KEYS]
        for cc in range(BAND_Q_CHUNKS):
            left = jnp.full((nh, cc * CHUNK), NEG_INF, F32)
            right = jnp.full((nh, BAND_W - BAND_KEYS - cc * CHUNK), NEG_INF, F32)
            parts = [p for p in (left, vals, right) if p.shape[1] > 0]
            o_ref[0, cc, rr] = jnp.concatenate(parts, axis=1)


def _band_bias(rel_bias):
    depth, nh, nrel = rel_bias.shape
    tab = jnp.pad(rel_bias, ((0, 0), (0, 0), (0, REL_PAD - nrel)))
    out = pl.pallas_call(
        _band_bias_kernel,
        grid=(depth, CHUNK // BIAS_ROWS_PER_STEP),
        in_specs=[pl.BlockSpec((1, nh, REL_PAD), lambda l, i: (l, 0, 0))],
        out_specs=pl.BlockSpec((1, BAND_Q_CHUNKS, BIAS_ROWS_PER_STEP, nh, BAND_W),
                               lambda l, i: (l, 0, i, 0, 0)),
        out_shape=jax.ShapeDtypeStruct((depth, BAND_Q_CHUNKS, CHUNK, nh, BAND_W), F32),
        compiler_params=_cparams(2),
        name="band_bias",
    )(tab)
    out = out.reshape(depth, BAND_TQ, nh, BAND_W)
    return jnp.transpose(out, (0, 2, 1, 3))


N_W1 = 6 * W_BR
N_W2 = 256
N_WT = 840
ROW_BIQ = W_BR
ROW_BV = W_BR + H_IDX * D_IDX
ROW_BIW = ROW_BV + DKV_B


def _in_proj_kernel(x_ref, mod_ref, w1_ref, w2_ref, wt_ref,
                    aq_ref, ak_ref, av_ref, akb_ref, avb_ref, cq_ref, ckb_ref, cvb_ref,
                    ckt_ref, cvt_ref, bk_ref, bv_ref, bik_ref, bkb_ref, bikb_ref,
                    bqt_ref, biqt_ref, bvt_ref, biwt_ref, *, first_tail_step):
    x = x_ref[0]
    d = x.shape[-1]
    tm = x.shape[0]
    mod = mod_ref[0]
    shift = mod[:, :d]
    scale = mod[:, d:2 * d]
    h = (_layer_norm(x) * (1.0 + scale) + shift).astype(BF16)

    def proj(j):
        return _dot(h, w1_ref[:, j * W_BR:(j + 1) * W_BR])

    aq_ref[0] = proj(0).astype(BF16)
    ak = proj(1)
    ak_ref[0] = ak
    akb_ref[0] = ak.astype(BF16)
    av = proj(2)
    av_ref[0] = av
    avb_ref[0] = av.astype(BF16)
    cq_ref[0] = proj(3).astype(BF16)
    ck = proj(4)
    ckb_ref[0] = ck.astype(BF16)
    cv = proj(5)
    cvb_ref[0] = cv.astype(BF16)

    @pl.when(pl.program_id(1) >= first_tail_step)
    def _():
        ckt_ref[0] = ck
        cvt_ref[0] = cv

    small = _dot(h, w2_ref[...])
    bk = small[:, 0:DKV_B]
    bik = small[:, 2 * DKV_B:3 * DKV_B]
    bk_ref[0] = bk
    bv_ref[0] = small[:, DKV_B:2 * DKV_B]
    bik_ref[0] = bik
    bkb_ref[0] = bk.astype(BF16)
    bikb_ref[0] = bik.astype(BF16)

    tr = _dot_nt(wt_ref[...], h)
    bqt_ref[0] = tr[0:ROW_BIQ].astype(BF16)
    biqt_ref[0] = tr[ROW_BIQ:ROW_BV].astype(BF16)
    bvt = tr[ROW_BV:ROW_BIW].astype(BF16)
    for c in range(tm // KEY_BLOCK):
        bvt_ref[0, c] = bvt[:, c * KEY_BLOCK:(c + 1) * KEY_BLOCK]
    biwt_ref[0] = tr[ROW_BIW:N_WT]


def _in_proj(x, mod, w1, w2, wt, tm):
    b, t, d = x.shape
    nt = t // tm
    tail = min(BAND_ROWS, t)
    tail_blocks = tail // tm
    first_tail = nt - tail_blocks
    row = lambda bb, i: (bb, i, 0)
    colmaj = lambda bb, i: (bb, 0, i)
    tail_map = lambda bb, i: (bb, jnp.maximum(i - first_tail, 0), 0)

    def rows(n, dt):
        return pl.BlockSpec((1, tm, n), row), jax.ShapeDtypeStruct((b, t, n), dt)

    outs = [
        rows(W_BR, BF16),
        rows(W_BR, F32), rows(W_BR, F32),
        rows(W_BR, BF16), rows(W_BR, BF16),
        rows(W_BR, BF16), rows(W_BR, BF16), rows(W_BR, BF16),
        (pl.BlockSpec((1, tm, W_BR), tail_map), jax.ShapeDtypeStruct((b, tail, W_BR), F32)),
        (pl.BlockSpec((1, tm, W_BR), tail_map), jax.ShapeDtypeStruct((b, tail, W_BR), F32)),
        rows(DKV_B, F32), rows(DKV_B, F32), rows(D_IDX, F32),
        rows(DKV_B, BF16), rows(D_IDX, BF16),
        (pl.BlockSpec((1, W_BR, tm), colmaj), jax.ShapeDtypeStruct((b, W_BR, t), BF16)),
        (pl.BlockSpec((1, H_IDX * D_IDX, tm), colmaj),
         jax.ShapeDtypeStruct((b, H_IDX * D_IDX, t), BF16)),
        (pl.BlockSpec((1, tm // KEY_BLOCK, DKV_B, KEY_BLOCK), lambda bb, i: (bb, i, 0, 0)),
         jax.ShapeDtypeStruct((b, t // KEY_BLOCK, DKV_B, KEY_BLOCK), BF16)),
        (pl.BlockSpec((1, N_WT - ROW_BIW, tm), colmaj),
         jax.ShapeDtypeStruct((b, N_WT - ROW_BIW, t), F32)),
    ]
    return pl.pallas_call(
        functools.partial(_in_proj_kernel, first_tail_step=first_tail),
        grid=(b, nt),
        in_specs=[pl.BlockSpec((1, tm, d), row),
                  pl.BlockSpec((1, 1, mod.shape[-1]), lambda bb, i: (bb, 0, 0)),
                  pl.BlockSpec((d, N_W1), lambda bb, i: (0, 0)),
                  pl.BlockSpec((d, N_W2), lambda bb, i: (0, 0)),
                  pl.BlockSpec((N_WT, d), lambda bb, i: (0, 0))],
        out_specs=[o[0] for o in outs],
        out_shape=[o[1] for o in outs],
        compiler_params=_cparams(2),
        name="in_proj",
    )(x, mod, w1, w2, wt)


LOG2E = 1.4426950408889634
STICK_TQ = 256
STICK_RUN = 2


def _stick_kernel(q_ref, kd_ref, vd_ref, km_ref, vm_ref, o_ref, carry_ref, acc_ref, *, n_main):
    tq = q_ref.shape[1]
    tk = KEY_BLOCK
    n_diag = tq // tk
    qb = pl.program_id(1)
    n_blocks = qb * n_diag if n_main is None else n_main

    jj = lax.broadcasted_iota(jnp.int32, (tk, tk), 0)
    ss = lax.broadcasted_iota(jnp.int32, (tk, tk), 1)
    upper = jnp.where(jj >= ss, 1.0, 0.0).astype(BF16)
    uo = jnp.concatenate([upper, jnp.ones((tk, tk), BF16)], axis=1)
    uu = jnp.concatenate([uo, uo], axis=0)
    qi = lax.broadcasted_iota(jnp.int32, (tq, tk), 0)
    kj = lax.broadcasted_iota(jnp.int32, (tq, tk), 1)

    carry_ref[...] = jnp.zeros(carry_ref.shape, F32)
    acc_ref[...] = jnp.zeros(acc_ref.shape, F32)

    heads = range(N_HEADS)
    lanes = [slice(h * D_HEAD, (h + 1) * D_HEAD) for h in heads]

    def step(kblks, vblks, visibles):
        nb = len(kblks)
        zs = [[_dot_nt(q_ref[0, :, lanes[h]], kblks[b][:, lanes[h]]) for h in heads]
              for b in range(nb)]
        splits = [[None] * N_HEADS for _ in range(nb)]
        for b in range(nb):
            for h in heads:
                sp = jnp.maximum(zs[b][h], 0.0) + jnp.log(1.0 + jnp.exp(-jnp.abs(zs[b][h])))
                if visibles[b] is not None:
                    sp = jnp.where(visibles[b], sp, 0.0)
                hi = sp.astype(BF16)
                lo = (sp - hi.astype(F32)).astype(BF16)
                splits[b][h] = jnp.concatenate([hi, lo], axis=1)
        srs = [[_dot(splits[b][h], uu) for h in heads] for b in range(nb)]
        weights = []
        for h in heads:
            later = carry_ref[h]
            w_h = []
            for b in range(nb):
                a = jnp.exp(zs[b][h] - srs[b][h][:, :tk] - later)
                if visibles[b] is not None:
                    a = jnp.where(visibles[b], a, 0.0)
                w_h.append(a.astype(BF16))
                later = later + srs[b][h][:, tk:]
            carry_ref[h] = later
            weights.append(jnp.concatenate(w_h, axis=1))
        vcat = jnp.concatenate(vblks, axis=0)
        for h in heads:
            acc_ref[h] += _dot(weights[h], vcat[:, lanes[h]])

    def diag_rows(dd):
        return slice(dd * tk, (dd + 1) * tk)

    diag = list(reversed(range(n_diag)))
    step([kd_ref[0, diag_rows(dd), :].astype(BF16) for dd in diag],
         [vd_ref[0, diag_rows(dd), :].astype(BF16) for dd in diag],
         [dd * tk + kj < qi for dd in diag])

    def key_rows(blk):
        return pl.ds(pl.multiple_of(blk * tk, tk), tk)

    def body(i, c):
        newest = n_blocks - 1 - STICK_RUN * i
        blks = [newest - j for j in range(STICK_RUN)]
        step([km_ref[0, key_rows(blk), :].astype(BF16) for blk in blks],
             [vm_ref[0, key_rows(blk), :].astype(BF16) for blk in blks], [None] * STICK_RUN)
        return c

    lax.fori_loop(0, n_blocks // STICK_RUN, body, 0)
    o_ref[0] = jnp.concatenate([acc_ref[h] for h in range(N_HEADS)], axis=1)


def _stick(q, k_diag, v_diag, k_main, v_main, *, tq, n_q_blocks, n_main):
    b = q.shape[0]
    tm = k_main.shape[1]
    blk = lambda bb, i: (bb, i, 0)
    whole = lambda bb, i: (bb, 0, 0)
    return pl.pallas_call(
        functools.partial(_stick_kernel, n_main=n_main),
        grid=(b, n_q_blocks),
        in_specs=[pl.BlockSpec((1, tq, W_BR), blk),
                  pl.BlockSpec((1, tq, W_BR), blk),
                  pl.BlockSpec((1, tq, W_BR), blk),
                  pl.BlockSpec((1, tm, W_BR), whole),
                  pl.BlockSpec((1, tm, W_BR), whole)],
        out_specs=pl.BlockSpec((1, tq, W_BR), blk),
        out_shape=jax.ShapeDtypeStruct((b, n_q_blocks * tq, W_BR), F32),
        scratch_shapes=[pltpu.VMEM((N_HEADS, tq, KEY_BLOCK), F32),
                        pltpu.VMEM((N_HEADS, tq, D_HEAD), F32)],
        compiler_params=_cparams(2),
        name="stick_breaking",
    )(q, k_diag, v_diag, k_main, v_main)


def _flip_magnitude_if_negative(word):
    return word ^ (lax.shift_right_arithmetic(word, 31) & jnp.int32(0x7FFFFFFF))


def _key_to_float(key):
    return lax.bitcast_convert_type(_flip_magnitude_if_negative(key), F32)


def _float_to_key(x):
    return _flip_magnitude_if_negative(lax.bitcast_convert_type(x, jnp.int32))


DSA_TK = 2 * KEY_BLOCK
ALIBI_ROWS = 64
BISECT_GROUP = 4


def _dsa_kernel(biqt_ref, biwt_ref, bqt_ref, ik_ref, k_ref, vt_ref, o_ref, sc_ref,
                *, n_full, q_off, n_valid, n_sel):
    tq = bqt_ref.shape[2]
    tk = DSA_TK
    sub = tk // KEY_BLOCK
    qb = pl.program_id(1)
    n_full_blocks = (qb * tq) // tk if n_full is None else n_full
    n_blocks = n_full_blocks + 1
    iw = biwt_ref[0]
    q0 = q_off + qb * tq

    kloc = lax.broadcasted_iota(jnp.int32, (tk, tq), 0)
    qpos = q0 + lax.broadcasted_iota(jnp.int32, (tk, tq), 1)

    def score_block(kb):
        start = pl.multiple_of(kb * tk, tk)
        ikb = ik_ref[0, pl.ds(start, tk), :].astype(BF16)
        sc = jnp.zeros((tk, tq), F32)
        for h in range(H_IDX):
            lg = _dot(ikb, biqt_ref[0, h * D_IDX:(h + 1) * D_IDX, :])
            sc = sc + iw[h:h + 1, :] * jnp.maximum(lg, 0.0)
        return jnp.where(sc == 0.0, 0.0, sc)

    def fill(j, c):
        kbs = [jnp.minimum(2 * j + u, n_full_blocks - 1) for u in range(2)]
        scores = [score_block(kb) for kb in kbs]
        for kb, sc in zip(kbs, scores):
            sc_ref[kb] = sc
        return c

    lax.fori_loop(0, (n_full_blocks + 1) // 2, fill, 0)
    kpos_last = n_full_blocks * tk + kloc
    admissible = ((lax.shift_right_logical(kpos_last, 6) <= lax.shift_right_logical(qpos, 6))
                  & (kpos_last < n_valid))
    sc_ref[n_full_blocks] = jnp.where(admissible, score_block(n_full_blocks), -jnp.inf)
    sc_ref[n_blocks] = jnp.full((tk, tq), -jnp.inf, F32)
    n_pairs = (n_blocks + 1) // 2

    def count(pred):
        def body(kb, acc):
            for j in range(sub):
                x = sc_ref[kb, j * KEY_BLOCK:(j + 1) * KEY_BLOCK, :]
                acc = acc + jnp.where(pred(x), 1.0, 0.0)
            return acc
        acc = lax.fori_loop(0, n_blocks, body, jnp.zeros((KEY_BLOCK, tq), F32))
        return jnp.sum(acc, axis=0, keepdims=True)

    k_sel = jnp.float32(n_sel)

    n_ge0 = count(lambda x: x >= 0.0)
    n_gt0 = count(lambda x: x > 0.0)
    zero_thr = (n_ge0 >= k_sel) & (n_gt0 < k_sel)
    positive = n_gt0 >= k_sel
    assert tk >= n_sel
    gmax = lax.fori_loop(0, n_blocks, lambda kb, g: jnp.maximum(g, sc_ref[kb]),
                         jnp.full((tk, tq), -jnp.inf, F32))
    lo_b = jnp.maximum(_float_to_key(jnp.min(gmax, axis=0, keepdims=True)), KEY_LO)
    hi_b = jnp.minimum(_float_to_key(jnp.max(gmax, axis=0, keepdims=True)), KEY_HI - 1) + 1
    lo0 = jnp.where(zero_thr, 0, jnp.where(positive, jnp.maximum(lo_b, 1), lo_b))
    hi0 = jnp.where(zero_thr, 1, jnp.where(positive, hi_b, jnp.minimum(hi_b, 0)))

    def unresolved(lo, hi):
        return (hi - lo) != 1

    def bisect_cond(st):
        it, lo, hi = st
        return ((it < 32 // BISECT_GROUP)
                & (jnp.max(jnp.where(unresolved(lo, hi), 1.0, 0.0)) > 0.5))

    def bisect(st):
        it, lo, hi = st
        for _ in range(BISECT_GROUP):
            active = unresolved(lo, hi)
            mid = lo + lax.shift_right_logical(hi - lo, 1)
            n_ge = count(lambda x, mid=mid: x >= _key_to_float(mid))
            ge = n_ge >= k_sel
            exact = n_ge == k_sel
            lo, hi = (jnp.where(active & ge, mid, lo),
                      jnp.where(active, jnp.where(exact, mid + 1, jnp.where(ge, hi, mid)), hi))
        return it + 1, lo, hi

    _, lo, _ = lax.while_loop(bisect_cond, bisect, (jnp.int32(0), lo0, hi0))
    thr = _key_to_float(lo)
    need = k_sel - count(lambda x: x > thr)

    srow = lax.broadcasted_iota(jnp.int32, (tk, tk), 0)
    jcol = lax.broadcasted_iota(jnp.int32, (tk, tk), 1)
    before = jnp.where(jcol < srow, 1.0, 0.0).astype(BF16)

    def select(j, seen):
        xs = [sc_ref[2 * j + u] for u in range(2)]
        eqs = [x == thr for x in xs]
        eqfs = [jnp.where(eq, 1.0, 0.0) for eq in eqs]
        within = [_dot(before, eqf.astype(BF16)) for eqf in eqfs]
        for u in range(2):
            chosen = (xs[u] > thr) | (eqs[u] & (within[u] + seen < need))
            sc_ref[2 * j + u] = jnp.where(chosen, 0.0, NEG_INF)
            seen = seen + jnp.sum(eqfs[u], axis=0, keepdims=True)
        return seen

    lax.fori_loop(0, n_pairs, select, jnp.zeros((1, tq), F32))

    q_all = jnp.concatenate(
        [bqt_ref[0, h * D_HEAD:(h + 1) * D_HEAD, :] for h in range(N_HEADS)], axis=1)
    slope = jnp.concatenate(
        [jnp.full((1, tq), 2.0 ** -(h + 1), F32) for h in range(N_HEADS)], axis=1)
    qi_all = jnp.concatenate(
        [lax.broadcasted_iota(jnp.int32, (1, tq), 1)] * N_HEADS, axis=1).astype(F32)
    arow = lax.broadcasted_iota(jnp.int32, (ALIBI_ROWS, N_HEADS * tq), 0)
    q_extra = jnp.where(arow == 0, -slope * qi_all, jnp.where(arow <= 2, slope, 0.0))
    q_aug = jnp.concatenate([q_all, q_extra.astype(BF16)], axis=0)
    acol = lax.broadcasted_iota(jnp.int32, (tk, ALIBI_ROWS), 1)
    kj_f = lax.broadcasted_iota(jnp.int32, (tk, ALIBI_ROWS), 0).astype(F32)

    def softmax_step(st, s_heads, vblk_t):
        ms, ls, accs = st
        m_new = [jnp.maximum(ms[h], jnp.max(s_heads[h], axis=0, keepdims=True))
                 for h in range(N_HEADS)]
        ps = [jnp.exp(s_heads[h] - m_new[h]) for h in range(N_HEADS)]
        new_l, new_acc = [], []
        for h in range(N_HEADS):
            alpha = jnp.exp(ms[h] - m_new[h])
            new_l.append(alpha * ls[h] + jnp.sum(ps[h], axis=0, keepdims=True))
            new_acc.append(alpha * accs[h] + _dot(vblk_t, ps[h].astype(BF16)))
        return tuple(m_new), tuple(new_l), tuple(new_acc)

    def load_kv(kb):
        start = pl.multiple_of(kb * tk, tk)
        kblk = k_ref[0, pl.ds(start, tk), :].astype(BF16)
        vblk_t = jnp.concatenate(
            [vt_ref[0, kb * sub + j].astype(BF16) for j in range(sub)], axis=1)
        return kblk, vblk_t

    def full_block_logits(kb, mb):
        kblk, vblk_t = load_kv(kb)
        neg_delta = (kb * tk - q0).astype(F32)
        k_extra = jnp.where(acol == 0, 1.0,
                            jnp.where(acol == 1, kj_f, jnp.where(acol == 2, neg_delta, 0.0)))
        k_aug = jnp.concatenate([kblk, k_extra.astype(BF16)], axis=1)
        return [_dot(k_aug, q_aug[:, h * tq:(h + 1) * tq]) + mb for h in range(N_HEADS)], vblk_t

    def attend_pair(parts, st):
        s_heads = [jnp.concatenate([parts[0][0][h], parts[1][0][h]], axis=0)
                   for h in range(N_HEADS)]
        return softmax_step(st, s_heads, jnp.concatenate([parts[0][1], parts[1][1]], axis=1))

    def attend_full_pair(j, st):
        return attend_pair([full_block_logits(2 * j + u, sc_ref[2 * j + u]) for u in range(2)], st)

    st = lax.fori_loop(
        0, n_full_blocks // 2, attend_full_pair,
        (tuple(jnp.full((1, tq), -3e38, F32) for _ in range(N_HEADS)),
         tuple(jnp.zeros((1, tq), F32) for _ in range(N_HEADS)),
         tuple(jnp.zeros((DKV_B, tq), F32) for _ in range(N_HEADS))))

    odd = (n_full_blocks % 2) == 1
    spare = jnp.maximum(n_full_blocks - 1, 0)
    spare_part = full_block_logits(spare, jnp.where(odd, sc_ref[spare], NEG_INF))
    kblk, vblk_t = load_kv(n_full_blocks)
    dist = jnp.abs(qpos - kpos_last).astype(F32)
    mb = sc_ref[n_full_blocks]
    last_part = ([_dot(kblk, q_all[:, h * tq:(h + 1) * tq]) + (mb - (2.0 ** -(h + 1)) * dist)
                  for h in range(N_HEADS)], vblk_t)
    _, ls, accs = attend_pair([spare_part, last_part], st)

    o_ref[0] = jnp.concatenate([(accs[h] / ls[h]).T for h in range(N_HEADS)], axis=1)


def _dsa(biqt, biwt, bqt, ik, k, vt, *, n_q_blocks, n_full, q_off, n_valid, n_sel):
    b = bqt.shape[0]
    tq = KEY_BLOCK
    tk_total = ik.shape[1]
    colmaj = lambda bb, i: (bb, 0, i)
    whole3 = lambda bb, i: (bb, 0, 0)
    return pl.pallas_call(
        functools.partial(_dsa_kernel, n_full=n_full, q_off=q_off, n_valid=n_valid, n_sel=n_sel),
        grid=(b, n_q_blocks),
        in_specs=[pl.BlockSpec((1, H_IDX * D_IDX, tq), colmaj),
                  pl.BlockSpec((1, biwt.shape[1], tq), colmaj),
                  pl.BlockSpec((1, W_BR, tq), colmaj),
                  pl.BlockSpec((1, tk_total, D_IDX), whole3),
                  pl.BlockSpec((1, tk_total, DKV_B), whole3),
                  pl.BlockSpec((1, tk_total // KEY_BLOCK, DKV_B, KEY_BLOCK),
                               lambda bb, i: (bb, 0, 0, 0))],
        out_specs=pl.BlockSpec((1, tq, W_BR), lambda bb, i: (bb, i, 0)),
        out_shape=jax.ShapeDtypeStruct((b, n_q_blocks * tq, W_BR), F32),
        scratch_shapes=[pltpu.VMEM((tk_total // DSA_TK + 1, DSA_TK, tq), F32)],
        compiler_params=_cparams(2),
        name="dsa",
    )(biqt, biwt, bqt, ik, k, vt)


def _band_kernel(*refs, n_win, win_start_blocks, n_valid_win):
    q_ref = refs[0]
    k_refs = refs[1:1 + n_win]
    v_refs = refs[1 + n_win:1 + 2 * n_win]
    bias_ref = refs[1 + 2 * n_win]
    o_ref = refs[2 + 2 * n_win]
    tq = q_ref.shape[1]
    w = n_win * tq
    i = pl.program_id(1)
    col = lax.broadcasted_iota(jnp.int32, (1, w), 1)
    if win_start_blocks is None:
        valid = col < n_valid_win
    else:
        valid = ((i + win_start_blocks) * tq + col >= 0) & (col < n_valid_win)
    kwin = jnp.concatenate([r[0] for r in k_refs], axis=0)
    vwin = jnp.concatenate([r[0] for r in v_refs], axis=0)
    outs = []
    for h in range(N_HEADS):
        lanes = slice(h * D_HEAD, (h + 1) * D_HEAD)
        s = _dot_nt(q_ref[0, :, lanes], kwin[:, lanes]) + bias_ref[h]
        s = jnp.where(valid, s, NEG_INF)
        e = jnp.exp(s - jnp.max(s, axis=-1, keepdims=True))
        den = jnp.sum(e, axis=-1, keepdims=True)
        outs.append(_dot(e.astype(BF16), vwin[:, lanes]) / den)
    o_ref[0] = jnp.concatenate(outs, axis=1)


def _band(q, k, v, bias, *, tq, n_q_blocks, sliding, n_valid_win):
    b = q.shape[0]
    n_win = bias.shape[2] // tq
    assert bias.shape[1] == tq

    def kv_map(wi):
        if sliding:
            return lambda bb, i: (bb, jnp.maximum(i - (n_win - 1) + wi, 0), 0)
        return lambda bb, i: (bb, wi, 0)

    kv_specs = [pl.BlockSpec((1, tq, W_BR), kv_map(wi)) for wi in range(n_win)]
    return pl.pallas_call(
        functools.partial(_band_kernel, n_win=n_win,
                          win_start_blocks=-(n_win - 1) if sliding else None,
                          n_valid_win=n_valid_win),
        grid=(b, n_q_blocks),
        in_specs=([pl.BlockSpec((1, tq, W_BR), lambda bb, i: (bb, i, 0))] + kv_specs + kv_specs
                  + [pl.BlockSpec(bias.shape, lambda bb, i: (0, 0, 0))]),
        out_specs=pl.BlockSpec((1, tq, W_BR), lambda bb, i: (bb, i, 0)),
        out_shape=jax.ShapeDtypeStruct((b, n_q_blocks * tq, W_BR), F32),
        compiler_params=_cparams(2),
        name="band",
    )(q, *([k] * n_win), *([v] * n_win), bias)


def _merge_kernel(x_ref, mod_ref, ya_ref, yb_ref, yc_ref, wz_ref, wg_ref, bg_ref, wbr_ref,
                  wo_ref, lng_ref, lnb_ref, o_ref, *, alpha):
    x = x_ref[0]
    d = x.shape[-1]
    mod = mod_ref[0]
    shift = mod[:, :d]
    scale = mod[:, d:2 * d]
    gate = mod[:, 2 * d:3 * d]
    h = (_layer_norm(x) * (1.0 + scale) + shift).astype(BF16)
    m = jnp.zeros(x.shape, F32)
    for n, y_ref in enumerate((ya_ref, yb_ref, yc_ref)):
        z = _dot(h, wz_ref[:, n * W_BR:(n + 1) * W_BR])
        u = (y_ref[0] * (z * _sigmoid(z))).astype(BF16)
        t = _dot(u, wbr_ref[n])
        g = _sigmoid(_dot(h, wg_ref[n]) + bg_ref[n])
        m = m + g * t
    out = _dot(m.astype(BF16), wo_ref[...])
    r = alpha * x + gate * out
    o_ref[0] = _layer_norm(r) * lng_ref[...] + lnb_ref[...]


def _merge(x, mod, ya, yb, yc, wz, wg, bg, wbr, wo, lng, lnb, *, tm, alpha):
    b, t, d = x.shape
    row = lambda bb, i: (bb, i, 0)
    const2 = lambda bb, i: (0, 0)
    const3 = lambda bb, i: (0, 0, 0)
    return pl.pallas_call(
        functools.partial(_merge_kernel, alpha=alpha),
        grid=(b, t // tm),
        in_specs=[pl.BlockSpec((1, tm, d), row),
                  pl.BlockSpec((1, 1, mod.shape[-1]), lambda bb, i: (bb, 0, 0)),
                  pl.BlockSpec((1, tm, W_BR), row),
                  pl.BlockSpec((1, tm, W_BR), row),
                  pl.BlockSpec((1, tm, W_BR), row),
                  pl.BlockSpec(wz.shape, const2),
                  pl.BlockSpec(wg.shape, const3),
                  pl.BlockSpec(bg.shape, const3),
                  pl.BlockSpec(wbr.shape, const3),
                  pl.BlockSpec(wo.shape, const2),
                  pl.BlockSpec(lng.shape, const2),
                  pl.BlockSpec(lnb.shape, const2)],
        out_specs=pl.BlockSpec((1, tm, d), row),
        out_shape=jax.ShapeDtypeStruct((b, t, d), F32),
        compiler_params=_cparams(2),
        name="merge",
    )(x, mod, ya, yb, yc, wz, wg, bg, wbr, wo, lng, lnb)


def _split_w_in(w):
    sizes = (W_BR, W_BR, W_BR, W_BR, W_BR, DKV_B, DKV_B, W_BR, H_IDX * D_IDX, D_IDX, H_IDX,
             W_BR, W_BR, W_BR, W_BR)
    cols, off = [], 0
    for s in sizes:
        cols.append(w[:, off:off + s])
        off += s
    aq, ak, av, az, bq, bk, bv, bz, biq, bik, biw, cq, ck, cv, cz = cols
    d = w.shape[0]
    w1 = jnp.concatenate([aq * ATT_SCALE, ak, av, cq * ATT_SCALE, ck, cv], axis=1).astype(BF16)
    w2 = jnp.concatenate([bk, bv, bik, jnp.zeros((d, N_W2 - 3 * DKV_B), w.dtype)],
                         axis=1).astype(BF16)
    wt = jnp.concatenate([bq * ATT_SCALE, biq * (D_IDX ** -0.5), bv, biw * IDX_HEAD_SCALE,
                          jnp.zeros((d, N_WT - ROW_BIW - H_IDX), w.dtype)], axis=1).T.astype(BF16)
    wz = jnp.concatenate([az, bz, cz], axis=1).astype(BF16)
    return w1, w2, wt, wz


def _key_blocks_t(v):
    b, t, n = v.shape
    return jnp.transpose(v.reshape(b, t // KEY_BLOCK, KEY_BLOCK, n), (0, 1, 3, 2))


def kernel(x_prompt, x_sample, c_prompt, c_sample, cache_a_k, cache_a_v, cache_b_k, cache_b_v,
           cache_b_kidx, cache_c_k, cache_c_v, w_ada, b_ada, w_in, w_gate, b_gate, w_branch,
           w_out, rel_bias, ln_g, ln_b):
    depth = w_in.shape[0]
    bp, seq, d = x_prompt.shape
    bs, dec_seq, _ = x_sample.shape
    past_len = cache_a_k.shape[2]
    band_rows = cache_c_k.shape[2]
    alpha = (2.0 * depth) ** 0.25
    assert seq % 512 == 0 and dec_seq <= CHUNK and past_len % DSA_TK == 0
    assert band_rows == BAND_ROWS and past_len // CHUNK == (past_len + dec_seq - 1) // CHUNK

    mod_all = _modulation(jnp.concatenate([c_prompt, c_sample], axis=0), w_ada, b_ada)
    bias_all = _band_bias(rel_bias)

    xp = x_prompt
    xs = jnp.pad(x_sample, ((0, 0), (0, SAMPLE_PAD - dec_seq), (0, 0)))
    n_sel_p = min(TOPK_MAX, seq // 4)
    n_sel_s = min(TOPK_MAX, (past_len + dec_seq) // 4)
    tm_p = 512
    st_p = [[] for _ in range(7)]
    st_s = [[] for _ in range(7)]

    for l in range(depth):
        w1, w2, wt, wz = _split_w_in(w_in[l])
        wg = w_gate[l].astype(BF16)
        bg = b_gate[l].reshape(N_BRANCH, 1, d)
        wbr = w_branch[l].astype(BF16)
        wo = w_out[l].astype(BF16)
        lng = ln_g[l].reshape(1, d)
        lnb = ln_b[l].reshape(1, d)
        mod_p = mod_all[l, :bp].reshape(bp, 1, 3 * d)
        mod_s = mod_all[l, bp:].reshape(bs, 1, 3 * d)
        bias_p = bias_all[l]

        (aq, ak, av, akb, avb, cq, ckb, cvb, ckt, cvt, bk, bv, bik, bkb, bikb,
         bqt, biqt, bvt, biwt) = _in_proj(xp, mod_p, w1, w2, wt, tm_p)
        ya = _stick(aq, akb, avb, akb, avb, tq=STICK_TQ, n_q_blocks=seq // STICK_TQ, n_main=None)
        yb = _dsa(biqt, biwt, bqt, bikb, bkb, bvt, n_q_blocks=seq // KEY_BLOCK, n_full=None,
                  q_off=0, n_valid=seq, n_sel=n_sel_p)
        yc = _band(cq, ckb, cvb, bias_p, tq=BAND_TQ, n_q_blocks=seq // BAND_TQ, sliding=True,
                   n_valid_win=BAND_W)
        xp = _merge(xp, mod_p, ya, yb, yc, wz, wg, bg, wbr, wo, lng, lnb, tm=256, alpha=alpha)
        for lst, val in zip(st_p, (ak.reshape(bp, seq, N_HEADS, D_HEAD),
                                   av.reshape(bp, seq, N_HEADS, D_HEAD), bk, bv, bik,
                                   ckt.reshape(bp, -1, N_HEADS, D_HEAD),
                                   cvt.reshape(bp, -1, N_HEADS, D_HEAD))):
            lst.append(val)

        (aq, ak, av, akb, avb, cq, ckb, cvb, ckt, cvt, bk, bv, bik, bkb, bikb,
         bqt, biqt, bvt, biwt) = _in_proj(xs, mod_s, w1, w2, wt, SAMPLE_PAD)
        ya = _stick(aq, akb, avb, cache_a_k[l].reshape(bs, past_len, W_BR),
                    cache_a_v[l].reshape(bs, past_len, W_BR),
                    tq=SAMPLE_PAD, n_q_blocks=1, n_main=past_len // KEY_BLOCK)
        key_pad = jnp.zeros((bs, DSA_TK - SAMPLE_PAD, DKV_B), BF16)
        ik_cat = jnp.concatenate([cache_b_kidx[l].astype(BF16), bikb, key_pad], axis=1)
        k_cat = jnp.concatenate([cache_b_k[l].astype(BF16), bkb, key_pad], axis=1)
        vt_cat = jnp.concatenate([_key_blocks_t(cache_b_v[l].astype(BF16)), bvt,
                                  _key_blocks_t(key_pad)], axis=1)
        yb = _dsa(biqt, biwt, bqt, ik_cat, k_cat, vt_cat, n_q_blocks=1,
                  n_full=past_len // DSA_TK, q_off=past_len, n_valid=past_len + dec_seq,
                  n_sel=n_sel_s)
        kc_cat = jnp.concatenate(
            [cache_c_k[l].reshape(bs, band_rows, W_BR).astype(BF16), ckb], axis=1)
        vc_cat = jnp.concatenate(
            [cache_c_v[l].reshape(bs, band_rows, W_BR).astype(BF16), cvb], axis=1)
        yc = _band(cq, kc_cat, vc_cat, bias_p[:, :SAMPLE_PAD, :band_rows + SAMPLE_PAD],
                   tq=SAMPLE_PAD, n_q_blocks=1, sliding=False, n_valid_win=band_rows + dec_seq)
        xs = _merge(xs, mod_s, ya, yb, yc, wz, wg, bg, wbr, wo, lng, lnb, tm=SAMPLE_PAD,
                    alpha=alpha)
        new_ck = jnp.concatenate(
            [cache_c_k[l], ckt[:, :dec_seq].reshape(bs, dec_seq, N_HEADS, D_HEAD)],
            axis=1)[:, -band_rows:]
        new_cv = jnp.concatenate(
            [cache_c_v[l], cvt[:, :dec_seq].reshape(bs, dec_seq, N_HEADS, D_HEAD)],
            axis=1)[:, -band_rows:]
        for lst, val in zip(st_s, (ak[:, :dec_seq].reshape(bs, dec_seq, N_HEADS, D_HEAD),
                                   av[:, :dec_seq].reshape(bs, dec_seq, N_HEADS, D_HEAD),
                                   bk[:, :dec_seq], bv[:, :dec_seq], bik[:, :dec_seq],
                                   new_ck, new_cv)):
            lst.append(val)

    return (xp, xs[:, :dec_seq], *[jnp.stack(v) for v in st_p], *[jnp.stack(v) for v in st_s])
```

```python
import functools

import jax
import jax.numpy as jnp
from jax import lax
from jax.experimental import pallas as pl
from jax.experimental.pallas import tpu as pltpu

F32 = jnp.float32
BF16 = jnp.bfloat16

CHUNK = 64
D_HEAD = 64
N_HEADS = 8
W_BR = N_HEADS * D_HEAD
DKV_B = 64
H_IDX = 4
D_IDX = 64
TOPK_MAX = 256
BAND_CHUNKS = 8
BAND_ROWS = BAND_CHUNKS * CHUNK
REL_CLIP = 128
N_BRANCH = 3
LN_EPS = 1e-5
NEG_INF = -1e30
ATT_SCALE = D_HEAD ** -0.5
IDX_HEAD_SCALE = H_IDX ** -0.5

LANE = 128
KEY_BLOCK = 128
SAMPLE_PAD = 128
VMEM_LIMIT = 56 * 1024 * 1024

KEY_LO = -2139095040
KEY_HI = 2139095040


def _cparams(n_axes):
    return pltpu.CompilerParams(
        dimension_semantics=("arbitrary",) * n_axes, vmem_limit_bytes=VMEM_LIMIT)


def _sigmoid(v):
    return 1.0 / (1.0 + jnp.exp(-v))


def _layer_norm(x):
    mu = jnp.mean(x, axis=-1, keepdims=True)
    xc = x - mu
    var = jnp.mean(xc * xc, axis=-1, keepdims=True)
    return xc * lax.rsqrt(var + LN_EPS)


def _dot(a, b):
    return jnp.dot(a, b, preferred_element_type=F32)


def _dot_nt(a, b):
    return lax.dot_general(a, b, (((1,), (1,)), ((), ())), preferred_element_type=F32)


def _mod_kernel(c_ref, w_ref, b_ref, o_ref):
    c = c_ref[...]
    s = c * _sigmoid(c)
    o_ref[0] = jnp.dot(s, w_ref[0], preferred_element_type=F32,
                       precision=lax.Precision.HIGHEST) + b_ref[0]


def _modulation(c_all, w_ada, b_ada):
    depth, d, d3 = w_ada.shape
    n = c_all.shape[0]
    tn = 1024
    return pl.pallas_call(
        _mod_kernel,
        grid=(depth, d3 // tn),
        in_specs=[pl.BlockSpec((n, d), lambda l, j: (0, 0)),
                  pl.BlockSpec((1, d, tn), lambda l, j: (l, 0, j)),
                  pl.BlockSpec((1, 1, tn), lambda l, j: (l, 0, j))],
        out_specs=pl.BlockSpec((1, n, tn), lambda l, j: (l, 0, j)),
        out_shape=jax.ShapeDtypeStruct((depth, n, d3), F32),
        compiler_params=_cparams(2),
        name="modulation",
    )(c_all, w_ada, b_ada.reshape(depth, 1, d3))


BAND_TQ = 256
BAND_W = BAND_ROWS + BAND_TQ
BIAS_ROWS_PER_STEP = 8
REL_PAD = 384


BAND_KEYS = BAND_ROWS + CHUNK
BAND_KEYS_PAD = 640
BAND_Q_CHUNKS = BAND_TQ // CHUNK


def _band_bias_kernel(tab_ref, o_ref):
    q0 = pl.program_id(1) * BIAS_ROWS_PER_STEP
    tab = tab_ref[0]
    col = lax.broadcasted_iota(jnp.int32, (1, BAND_KEYS_PAD), 1)
    sub = lax.broadcasted_iota(jnp.int32, (REL_PAD, BAND_KEYS_PAD), 0)
    nh = tab.shape[0]
    for rr in range(BIAS_ROWS_PER_STEP):
        idx = jnp.clip(q0 + rr - col + BAND_ROWS, -REL_CLIP, REL_CLIP) + REL_CLIP
        onehot = jnp.where(sub == idx, 1.0, 0.0).astype(F32)
        vals = jnp.dot(tab, onehot, preferred_element_type=F32,
                       precision=lax.Precision.HIGHEST)[:, :BAND_KEYS]
        for cc in range(BAND_Q_CHUNKS):
            left = jnp.full((nh, cc * CHUNK), NEG_INF, F32)
            right = jnp.full((nh, BAND_W - BAND_KEYS - cc * CHUNK), NEG_INF, F32)
            parts = [p for p in (left, vals, right) if p.shape[1] > 0]
            o_ref[0, cc, rr] = jnp.concatenate(parts, axis=1)


def _band_bias(rel_bias):
    depth, nh, nrel = rel_bias.shape
    tab = jnp.pad(rel_bias, ((0, 0), (0, 0), (0, REL_PAD - nrel)))
    out = pl.pallas_call(
        _band_bias_kernel,
        grid=(depth, CHUNK // BIAS_ROWS_PER_STEP),
        in_specs=[pl.BlockSpec((1, nh, REL_PAD), lambda l, i: (l, 0, 0))],
        out_specs=pl.BlockSpec((1, BAND_Q_CHUNKS, BIAS_ROWS_PER_STEP, nh, BAND_W),
                               lambda l, i: (l, 0, i, 0, 0)),
        out_shape=jax.ShapeDtypeStruct((depth, BAND_Q_CHUNKS, CHUNK, nh, BAND_W), F32),
        compiler_params=_cparams(2),
        name="band_bias",
    )(tab)
    out = out.reshape(depth, BAND_TQ, nh, BAND_W)
    return jnp.transpose(out, (0, 2, 1, 3))


N_W1 = 6 * W_BR
N_W2 = 256
N_WT = 840
ROW_BIQ = W_BR
ROW_BV = W_BR + H_IDX * D_IDX
ROW_BIW = ROW_BV + DKV_B


def _in_proj_kernel(x_ref, mod_ref, w1_ref, w2_ref, wt_ref,
                    aq_ref, ak_ref, av_ref, akb_ref, avb_ref, cq_ref, ckb_ref, cvb_ref,
                    ckt_ref, cvt_ref, bk_ref, bv_ref, bik_ref, bkb_ref, bikb_ref,
                    bqt_ref, biqt_ref, bvt_ref, biwt_ref, *, first_tail_step):
    x = x_ref[0]
    d = x.shape[-1]
    tm = x.shape[0]
    mod = mod_ref[0]
    shift = mod[:, :d]
    scale = mod[:, d:2 * d]
    h = (_layer_norm(x) * (1.0 + scale) + shift).astype(BF16)

    def proj(j):
        return _dot(h, w1_ref[:, j * W_BR:(j + 1) * W_BR])

    aq_ref[0] = proj(0).astype(BF16)
    ak = proj(1)
    ak_ref[0] = ak
    akb_ref[0] = ak.astype(BF16)
    av = proj(2)
    av_ref[0] = av
    avb_ref[0] = av.astype(BF16)
    cq_ref[0] = proj(3).astype(BF16)
    ck = proj(4)
    ckb_ref[0] = ck.astype(BF16)
    cv = proj(5)
    cvb_ref[0] = cv.astype(BF16)

    @pl.when(pl.program_id(1) >= first_tail_step)
    def _():
        ckt_ref[0] = ck
        cvt_ref[0] = cv

    small = _dot(h, w2_ref[...])
    bk = small[:, 0:DKV_B]
    bik = small[:, 2 * DKV_B:3 * DKV_B]
    bk_ref[0] = bk
    bv_ref[0] = small[:, DKV_B:2 * DKV_B]
    bik_ref[0] = bik
    bkb_ref[0] = bk.astype(BF16)
    bikb_ref[0] = bik.astype(BF16)

    tr = _dot_nt(wt_ref[...], h)
    bqt_ref[0] = tr[0:ROW_BIQ].astype(BF16)
    biqt_ref[0] = tr[ROW_BIQ:ROW_BV].astype(BF16)
    bvt = tr[ROW_BV:ROW_BIW].astype(BF16)
    for c in range(tm // KEY_BLOCK):
        bvt_ref[0, c] = bvt[:, c * KEY_BLOCK:(c + 1) * KEY_BLOCK]
    biwt_ref[0] = tr[ROW_BIW:N_WT]


def _in_proj(x, mod, w1, w2, wt, tm):
    b, t, d = x.shape
    nt = t // tm
    tail = min(BAND_ROWS, t)
    tail_blocks = tail // tm
    first_tail = nt - tail_blocks
    row = lambda bb, i: (bb, i, 0)
    colmaj = lambda bb, i: (bb, 0, i)
    tail_map = lambda bb, i: (bb, jnp.maximum(i - first_tail, 0), 0)

    def rows(n, dt):
        return pl.BlockSpec((1, tm, n), row), jax.ShapeDtypeStruct((b, t, n), dt)

    outs = [
        rows(W_BR, BF16),
        rows(W_BR, F32), rows(W_BR, F32),
        rows(W_BR, BF16), rows(W_BR, BF16),
        rows(W_BR, BF16), rows(W_BR, BF16), rows(W_BR, BF16),
        (pl.BlockSpec((1, tm, W_BR), tail_map), jax.ShapeDtypeStruct((b, tail, W_BR), F32)),
        (pl.BlockSpec((1, tm, W_BR), tail_map), jax.ShapeDtypeStruct((b, tail, W_BR), F32)),
        rows(DKV_B, F32), rows(DKV_B, F32), rows(D_IDX, F32),
        rows(DKV_B, BF16), rows(D_IDX, BF16),
        (pl.BlockSpec((1, W_BR, tm), colmaj), jax.ShapeDtypeStruct((b, W_BR, t), BF16)),
        (pl.BlockSpec((1, H_IDX * D_IDX, tm), colmaj),
         jax.ShapeDtypeStruct((b, H_IDX * D_IDX, t), BF16)),
        (pl.BlockSpec((1, tm // KEY_BLOCK, DKV_B, KEY_BLOCK), lambda bb, i: (bb, i, 0, 0)),
         jax.ShapeDtypeStruct((b, t // KEY_BLOCK, DKV_B, KEY_BLOCK), BF16)),
        (pl.BlockSpec((1, N_WT - ROW_BIW, tm), colmaj),
         jax.ShapeDtypeStruct((b, N_WT - ROW_BIW, t), F32)),
    ]
    return pl.pallas_call(
        functools.partial(_in_proj_kernel, first_tail_step=first_tail),
        grid=(b, nt),
        in_specs=[pl.BlockSpec((1, tm, d), row),
                  pl.BlockSpec((1, 1, mod.shape[-1]), lambda bb, i: (bb, 0, 0)),
                  pl.BlockSpec((d, N_W1), lambda bb, i: (0, 0)),
                  pl.BlockSpec((d, N_W2), lambda bb, i: (0, 0)),
                  pl.BlockSpec((N_WT, d), lambda bb, i: (0, 0))],
        out_specs=[o[0] for o in outs],
        out_shape=[o[1] for o in outs],
        compiler_params=_cparams(2),
        name="in_proj",
    )(x, mod, w1, w2, wt)


SOFTPLUS_CUT = 30.0
STICK_TQ = 256
STICK_RUN = 2


def _stick_kernel(q_ref, kd_ref, vd_ref, km_ref, vm_ref, o_ref, carry_ref, acc_ref, *, n_main):
    tq = q_ref.shape[1]
    tk = KEY_BLOCK
    n_diag = tq // tk
    qb = pl.program_id(1)
    n_blocks = qb * n_diag if n_main is None else n_main

    def suffix_matrix(n):
        jj = lax.broadcasted_iota(jnp.int32, (n, n), 0)
        ss = lax.broadcasted_iota(jnp.int32, (n, n), 1)
        return jnp.where(jj >= ss, 1.0, 0.0).astype(BF16)

    upper_of = {nb: suffix_matrix(nb * tk) for nb in sorted({n_diag, STICK_RUN})}
    qi = lax.broadcasted_iota(jnp.int32, (tq, tk), 0)
    kj = lax.broadcasted_iota(jnp.int32, (tq, tk), 1)

    carry_ref[...] = jnp.zeros(carry_ref.shape, F32)
    acc_ref[...] = jnp.zeros(acc_ref.shape, F32)

    heads = range(N_HEADS)
    pairs = range(N_HEADS // 2)
    pair_lanes = [slice(p * 2 * D_HEAD, (p + 1) * 2 * D_HEAD) for p in pairs]
    first_of_pair = lax.broadcasted_iota(jnp.int32, (tk, 2 * D_HEAD), 1) < D_HEAD

    def head_pair_diag(x):
        zero = jnp.zeros_like(x)
        return jnp.concatenate([jnp.where(first_of_pair, x, zero),
                                jnp.where(first_of_pair, zero, x)], axis=0)

    def step(kblks, vblks, visibles):
        nb = len(kblks)
        zs = [[None] * N_HEADS for _ in range(nb)]
        for b in range(nb):
            for p in pairs:
                z2 = _dot_nt(q_ref[0, :, pair_lanes[p]], head_pair_diag(kblks[b][:, pair_lanes[p]]))
                zs[b][2 * p], zs[b][2 * p + 1] = z2[:, :tk], z2[:, tk:]
        sps = [[None] * N_HEADS for _ in range(nb)]
        for b in range(nb):
            for h in heads:
                z = zs[b][h]
                sp = jnp.maximum(z, jnp.log(1.0 + jnp.exp(jnp.minimum(z, SOFTPLUS_CUT))))
                if visibles[b] is not None:
                    sp = jnp.where(visibles[b], sp, 0.0)
                sps[b][h] = sp.astype(BF16)
        order = list(reversed(range(nb)))
        upper = upper_of[nb]
        srs = [_dot(jnp.concatenate([sps[b][h] for b in order], axis=1), upper) for h in heads]
        run_visible = None
        if visibles[0] is not None:
            run_visible = jnp.concatenate([visibles[b] for b in order], axis=1)
        weights = []
        for h in heads:
            later = carry_ref[h]
            z_run = jnp.concatenate([zs[b][h] for b in order], axis=1)
            a = jnp.exp(z_run - srs[h] - jnp.concatenate([later] * nb, axis=1))
            if run_visible is not None:
                a = jnp.where(run_visible, a, 0.0)
            weights.append(a.astype(BF16))
            carry_ref[h] = later + jnp.broadcast_to(srs[h][:, 0:1], (tq, tk))
        for p in pairs:
            w2 = jnp.concatenate([weights[2 * p], weights[2 * p + 1]], axis=1)
            vd = [head_pair_diag(vblks[b][:, pair_lanes[p]]) for b in order]
            v2 = jnp.concatenate([v[:tk] for v in vd] + [v[tk:] for v in vd], axis=0)
            acc_ref[p] += _dot(w2, v2)

    def diag_rows(dd):
        return slice(dd * tk, (dd + 1) * tk)

    diag = list(reversed(range(n_diag)))
    step([kd_ref[0, diag_rows(dd), :].astype(BF16) for dd in diag],
         [vd_ref[0, diag_rows(dd), :].astype(BF16) for dd in diag],
         [dd * tk + kj < qi for dd in diag])

    def key_rows(blk):
        return pl.ds(pl.multiple_of(blk * tk, tk), tk)

    def body(i, c):
        newest = n_blocks - 1 - STICK_RUN * i
        blks = [newest - j for j in range(STICK_RUN)]
        step([km_ref[0, key_rows(blk), :].astype(BF16) for blk in blks],
             [vm_ref[0, key_rows(blk), :].astype(BF16) for blk in blks], [None] * STICK_RUN)
        return c

    lax.fori_loop(0, n_blocks // STICK_RUN, body, 0)
    o_ref[0] = jnp.concatenate([acc_ref[p] for p in pairs], axis=1)


def _stick(q, k_diag, v_diag, k_main, v_main, *, tq, n_q_blocks, n_main):
    b = q.shape[0]
    tm = k_main.shape[1]
    blk = lambda bb, i: (bb, i, 0)
    whole = lambda bb, i: (bb, 0, 0)
    return pl.pallas_call(
        functools.partial(_stick_kernel, n_main=n_main),
        grid=(b, n_q_blocks),
        in_specs=[pl.BlockSpec((1, tq, W_BR), blk),
                  pl.BlockSpec((1, tq, W_BR), blk),
                  pl.BlockSpec((1, tq, W_BR), blk),
                  pl.BlockSpec((1, tm, W_BR), whole),
                  pl.BlockSpec((1, tm, W_BR), whole)],
        out_specs=pl.BlockSpec((1, tq, W_BR), blk),
        out_shape=jax.ShapeDtypeStruct((b, n_q_blocks * tq, W_BR), F32),
        scratch_shapes=[pltpu.VMEM((N_HEADS, tq, KEY_BLOCK), F32),
                        pltpu.VMEM((N_HEADS // 2, tq, 2 * D_HEAD), F32)],
        compiler_params=_cparams(2),
        name="stick_breaking",
    )(q, k_diag, v_diag, k_main, v_main)


def _flip_magnitude_if_negative(word):
    return word ^ (lax.shift_right_arithmetic(word, 31) & jnp.int32(0x7FFFFFFF))


def _key_to_float(key):
    return lax.bitcast_convert_type(_flip_magnitude_if_negative(key), F32)


def _float_to_key(x):
    return _flip_magnitude_if_negative(lax.bitcast_convert_type(x, jnp.int32))


DSA_TK = 2 * KEY_BLOCK
ALIBI_ROWS = 64
BISECT_GROUP = 4
BISECT_FREE_GROUPS = 3


def _dsa_kernel(biqt_ref, biwt_ref, bqt_ref, ik_ref, k_ref, vt_ref, o_ref, sc_ref,
                *, n_full, q_off, n_valid, n_sel):
    tq = bqt_ref.shape[2]
    tk = DSA_TK
    sub = tk // KEY_BLOCK
    qb = pl.program_id(1)
    n_full_blocks = (qb * tq) // tk if n_full is None else n_full
    n_blocks = n_full_blocks + 1
    iw = biwt_ref[0]
    q0 = q_off + qb * tq

    kloc = lax.broadcasted_iota(jnp.int32, (tk, tq), 0)
    qpos = q0 + lax.broadcasted_iota(jnp.int32, (tk, tq), 1)

    def score_block(kb):
        start = pl.multiple_of(kb * tk, tk)
        ikb = ik_ref[0, pl.ds(start, tk), :].astype(BF16)
        sc = jnp.zeros((tk, tq), F32)
        for hp in range(H_IDX // 2):
            iq2 = jnp.concatenate(
                [biqt_ref[0, h * D_IDX:(h + 1) * D_IDX, :] for h in (2 * hp, 2 * hp + 1)], axis=1)
            lg2 = _dot(ikb, iq2)
            for u in range(2):
                h = 2 * hp + u
                sc = sc + iw[h:h + 1, :] * jnp.maximum(lg2[:, u * tq:(u + 1) * tq], 0.0)
        return jnp.where(sc == 0.0, 0.0, sc)

    def fill(j, c):
        kbs = [jnp.minimum(2 * j + u, n_full_blocks - 1) for u in range(2)]
        scores = [score_block(kb) for kb in kbs]
        for kb, sc in zip(kbs, scores):
            sc_ref[kb] = sc
        return c

    lax.fori_loop(0, (n_full_blocks + 1) // 2, fill, 0)
    kpos_last = n_full_blocks * tk + kloc
    admissible = ((lax.shift_right_logical(kpos_last, 6) <= lax.shift_right_logical(qpos, 6))
                  & (kpos_last < n_valid))
    sc_ref[n_full_blocks] = jnp.where(admissible, score_block(n_full_blocks), -jnp.inf)
    sc_ref[n_blocks] = jnp.full((tk, tq), -jnp.inf, F32)
    n_pairs = (n_blocks + 1) // 2

    def count(pred):
        def body(kb, acc):
            for j in range(sub):
                x = sc_ref[kb, j * KEY_BLOCK:(j + 1) * KEY_BLOCK, :]
                acc = acc + jnp.where(pred(x), 1.0, 0.0)
            return acc
        acc = lax.fori_loop(0, n_blocks, body, jnp.zeros((KEY_BLOCK, tq), F32))
        return jnp.sum(acc, axis=0, keepdims=True)

    k_sel = jnp.float32(n_sel)

    n_ge0 = count(lambda x: x >= 0.0)
    n_gt0 = count(lambda x: x > 0.0)
    zero_thr = (n_ge0 >= k_sel) & (n_gt0 < k_sel)
    positive = n_gt0 >= k_sel
    assert tk >= n_sel
    gmax = lax.fori_loop(0, n_blocks, lambda kb, g: jnp.maximum(g, sc_ref[kb]),
                         jnp.full((tk, tq), -jnp.inf, F32))
    lo_b = jnp.maximum(_float_to_key(jnp.min(gmax, axis=0, keepdims=True)), KEY_LO)
    hi_b = jnp.minimum(_float_to_key(jnp.max(gmax, axis=0, keepdims=True)), KEY_HI - 1) + 1
    lo0 = jnp.where(zero_thr, 0, jnp.where(positive, jnp.maximum(lo_b, 1), lo_b))
    hi0 = jnp.where(zero_thr, 1, jnp.where(positive, hi_b, jnp.minimum(hi_b, 0)))

    def unresolved(lo, hi):
        return (hi - lo) != 1

    def bisect_cond(st):
        it, lo, hi = st
        return ((it < 32 // BISECT_GROUP)
                & (jnp.max(jnp.where(unresolved(lo, hi), 1.0, 0.0)) > 0.5))

    def bisect(st):
        it, lo, hi = st
        for _ in range(BISECT_GROUP):
            active = unresolved(lo, hi)
            mid = lo + lax.shift_right_logical(hi - lo, 1)
            n_ge = count(lambda x, mid=mid: x >= _key_to_float(mid))
            ge = n_ge >= k_sel
            exact = n_ge == k_sel
            lo, hi = (jnp.where(active & ge, mid, lo),
                      jnp.where(active, jnp.where(exact, mid + 1, jnp.where(ge, hi, mid)), hi))
        return it + 1, lo, hi

    st = lax.fori_loop(0, BISECT_FREE_GROUPS, lambda _, s: bisect(s), (jnp.int32(0), lo0, hi0))
    _, lo, _ = lax.while_loop(bisect_cond, bisect, st)
    thr = _key_to_float(lo)
    need = k_sel - count(lambda x: x > thr)

    srow = lax.broadcasted_iota(jnp.int32, (tk, tk), 0)
    jcol = lax.broadcasted_iota(jnp.int32, (tk, tk), 1)
    before = jnp.where(jcol < srow, 1.0, 0.0).astype(BF16)

    def select(j, seen):
        xs = [sc_ref[2 * j + u] for u in range(2)]
        eqs = [x == thr for x in xs]
        eqfs = [jnp.where(eq, 1.0, 0.0) for eq in eqs]
        within = [_dot(before, eqf.astype(BF16)) for eqf in eqfs]
        for u in range(2):
            chosen = (xs[u] > thr) | (eqs[u] & (within[u] + seen < need))
            sc_ref[2 * j + u] = jnp.where(chosen, 0.0, NEG_INF)
            seen = seen + jnp.sum(eqfs[u], axis=0, keepdims=True)
        return seen

    lax.fori_loop(0, n_pairs, select, jnp.zeros((1, tq), F32))

    q_all = jnp.concatenate(
        [bqt_ref[0, h * D_HEAD:(h + 1) * D_HEAD, :] for h in range(N_HEADS)], axis=1)
    slope = jnp.concatenate(
        [jnp.full((1, tq), 2.0 ** -(h + 1), F32) for h in range(N_HEADS)], axis=1)
    qi_all = jnp.concatenate(
        [lax.broadcasted_iota(jnp.int32, (1, tq), 1)] * N_HEADS, axis=1).astype(F32)
    arow = lax.broadcasted_iota(jnp.int32, (ALIBI_ROWS, N_HEADS * tq), 0)
    q_extra = jnp.where(arow == 0, -slope * qi_all, jnp.where(arow <= 2, slope, 0.0))
    q_aug = jnp.concatenate([q_all, q_extra.astype(BF16)], axis=0)
    acol = lax.broadcasted_iota(jnp.int32, (tk, ALIBI_ROWS), 1)
    kj_f = lax.broadcasted_iota(jnp.int32, (tk, ALIBI_ROWS), 0).astype(F32)

    def load_kv(kb):
        start = pl.multiple_of(kb * tk, tk)
        kblk = k_ref[0, pl.ds(start, tk), :].astype(BF16)
        vblk_t = jnp.concatenate(
            [vt_ref[0, kb * sub + j].astype(BF16) for j in range(sub)], axis=1)
        return kblk, vblk_t

    def softmax_step(st, s_heads, vblk_t):
        ms, ls, accs = st
        m_new = [jnp.maximum(ms[h], jnp.max(s_heads[h], axis=0, keepdims=True))
                 for h in range(N_HEADS)]
        ps = [jnp.exp(s_heads[h] - m_new[h]) for h in range(N_HEADS)]
        new_l, new_acc = [], []
        for hp in range(N_HEADS // 2):
            pv2 = _dot(vblk_t, jnp.concatenate(
                [ps[2 * hp].astype(BF16), ps[2 * hp + 1].astype(BF16)], axis=1))
            for u in range(2):
                h = 2 * hp + u
                alpha = jnp.exp(ms[h] - m_new[h])
                new_l.append(alpha * ls[h] + jnp.sum(ps[h], axis=0, keepdims=True))
                new_acc.append(alpha * accs[h] + pv2[:, u * tq:(u + 1) * tq])
        return tuple(m_new), tuple(new_l), tuple(new_acc)

    def per_head_logits(keys, queries):
        tiles = []
        for hp in range(N_HEADS // 2):
            s2 = _dot(keys, queries[:, 2 * hp * tq:(2 * hp + 2) * tq])
            tiles += [s2[:, :tq], s2[:, tq:]]
        return tiles

    def full_block_logits(kb, mb):
        kblk, vblk_t = load_kv(kb)
        neg_delta = (kb * tk - q0).astype(F32)
        k_extra = jnp.where(acol == 0, 1.0,
                            jnp.where(acol == 1, kj_f, jnp.where(acol == 2, neg_delta, 0.0)))
        k_aug = jnp.concatenate([kblk, k_extra.astype(BF16)], axis=1)
        return [s + mb for s in per_head_logits(k_aug, q_aug)], vblk_t

    def attend_pair(parts, st):
        s_heads = [jnp.concatenate([parts[0][0][h], parts[1][0][h]], axis=0)
                   for h in range(N_HEADS)]
        return softmax_step(st, s_heads, jnp.concatenate([parts[0][1], parts[1][1]], axis=1))

    def attend_full_pair(j, st):
        return attend_pair([full_block_logits(2 * j + u, sc_ref[2 * j + u]) for u in range(2)], st)

    st = lax.fori_loop(
        0, n_full_blocks // 2, attend_full_pair,
        (tuple(jnp.full((1, tq), -3e38, F32) for _ in range(N_HEADS)),
         tuple(jnp.zeros((1, tq), F32) for _ in range(N_HEADS)),
         tuple(jnp.zeros((DKV_B, tq), F32) for _ in range(N_HEADS))))

    odd = (n_full_blocks % 2) == 1
    spare = jnp.maximum(n_full_blocks - 1, 0)
    spare_part = full_block_logits(spare, jnp.where(odd, sc_ref[spare], NEG_INF))
    kblk, vblk_t = load_kv(n_full_blocks)
    dist = jnp.abs(qpos - kpos_last).astype(F32)
    mb = sc_ref[n_full_blocks]
    last_part = ([s + (mb - (2.0 ** -(h + 1)) * dist)
                  for h, s in enumerate(per_head_logits(kblk, q_all))], vblk_t)
    _, ls, accs = attend_pair([spare_part, last_part], st)

    o_ref[0] = jnp.concatenate([(accs[h] / ls[h]).T for h in range(N_HEADS)], axis=1)


def _dsa(biqt, biwt, bqt, ik, k, vt, *, n_q_blocks, n_full, q_off, n_valid, n_sel):
    b = bqt.shape[0]
    tq = KEY_BLOCK
    tk_total = ik.shape[1]
    colmaj = lambda bb, i: (bb, 0, i)
    whole3 = lambda bb, i: (bb, 0, 0)
    return pl.pallas_call(
        functools.partial(_dsa_kernel, n_full=n_full, q_off=q_off, n_valid=n_valid, n_sel=n_sel),
        grid=(b, n_q_blocks),
        in_specs=[pl.BlockSpec((1, H_IDX * D_IDX, tq), colmaj),
                  pl.BlockSpec((1, biwt.shape[1], tq), colmaj),
                  pl.BlockSpec((1, W_BR, tq), colmaj),
                  pl.BlockSpec((1, tk_total, D_IDX), whole3),
                  pl.BlockSpec((1, tk_total, DKV_B), whole3),
                  pl.BlockSpec((1, tk_total // KEY_BLOCK, DKV_B, KEY_BLOCK),
                               lambda bb, i: (bb, 0, 0, 0))],
        out_specs=pl.BlockSpec((1, tq, W_BR), lambda bb, i: (bb, i, 0)),
        out_shape=jax.ShapeDtypeStruct((b, n_q_blocks * tq, W_BR), F32),
        scratch_shapes=[pltpu.VMEM((tk_total // DSA_TK + 1, DSA_TK, tq), F32)],
        compiler_params=_cparams(2),
        name="dsa",
    )(biqt, biwt, bqt, ik, k, vt)


def _band_kernel(*refs, n_win, win_start_blocks, n_valid_win):
    q_ref = refs[0]
    k_refs = refs[1:1 + n_win]
    v_refs = refs[1 + n_win:1 + 2 * n_win]
    bias_ref = refs[1 + 2 * n_win]
    o_ref = refs[2 + 2 * n_win]
    tq = q_ref.shape[1]
    w = n_win * tq
    i = pl.program_id(1)
    col = lax.broadcasted_iota(jnp.int32, (1, w), 1)
    if win_start_blocks is None:
        valid = col < n_valid_win
    else:
        valid = ((i + win_start_blocks) * tq + col >= 0) & (col < n_valid_win)
    kwin = jnp.concatenate([r[0] for r in k_refs], axis=0)
    vwin = jnp.concatenate([r[0] for r in v_refs], axis=0)
    outs = []
    for h in range(N_HEADS):
        lanes = slice(h * D_HEAD, (h + 1) * D_HEAD)
        s = _dot_nt(q_ref[0, :, lanes], kwin[:, lanes]) + bias_ref[h]
        s = jnp.where(valid, s, NEG_INF)
        e = jnp.exp(s - jnp.max(s, axis=-1, keepdims=True))
        den = jnp.sum(e, axis=-1, keepdims=True)
        outs.append(_dot(e.astype(BF16), vwin[:, lanes]) / den)
    o_ref[0] = jnp.concatenate(outs, axis=1)


def _band(q, k, v, bias, *, tq, n_q_blocks, sliding, n_valid_win):
    b = q.shape[0]
    n_win = bias.shape[2] // tq
    assert bias.shape[1] == tq

    def kv_map(wi):
        if sliding:
            return lambda bb, i: (bb, jnp.maximum(i - (n_win - 1) + wi, 0), 0)
        return lambda bb, i: (bb, wi, 0)

    kv_specs = [pl.BlockSpec((1, tq, W_BR), kv_map(wi)) for wi in range(n_win)]
    return pl.pallas_call(
        functools.partial(_band_kernel, n_win=n_win,
                          win_start_blocks=-(n_win - 1) if sliding else None,
                          n_valid_win=n_valid_win),
        grid=(b, n_q_blocks),
        in_specs=([pl.BlockSpec((1, tq, W_BR), lambda bb, i: (bb, i, 0))] + kv_specs + kv_specs
                  + [pl.BlockSpec(bias.shape, lambda bb, i: (0, 0, 0))]),
        out_specs=pl.BlockSpec((1, tq, W_BR), lambda bb, i: (bb, i, 0)),
        out_shape=jax.ShapeDtypeStruct((b, n_q_blocks * tq, W_BR), F32),
        compiler_params=_cparams(2),
        name="band",
    )(q, *([k] * n_win), *([v] * n_win), bias)


def _merge_kernel(x_ref, mod_ref, ya_ref, yb_ref, yc_ref, wz_ref, wg_ref, bg_ref, wbr_ref,
                  wo_ref, lng_ref, lnb_ref, o_ref, *, alpha):
    x = x_ref[0]
    d = x.shape[-1]
    mod = mod_ref[0]
    shift = mod[:, :d]
    scale = mod[:, d:2 * d]
    gate = mod[:, 2 * d:3 * d]
    h = (_layer_norm(x) * (1.0 + scale) + shift).astype(BF16)
    m = jnp.zeros(x.shape, F32)
    for n, y_ref in enumerate((ya_ref, yb_ref, yc_ref)):
        z = _dot(h, wz_ref[:, n * W_BR:(n + 1) * W_BR])
        u = (y_ref[0] * (z * _sigmoid(z))).astype(BF16)
        t = _dot(u, wbr_ref[n])
        g = _sigmoid(_dot(h, wg_ref[n]) + bg_ref[n])
        m = m + g * t
    out = _dot(m.astype(BF16), wo_ref[...])
    r = alpha * x + gate * out
    o_ref[0] = _layer_norm(r) * lng_ref[...] + lnb_ref[...]


def _merge(x, mod, ya, yb, yc, wz, wg, bg, wbr, wo, lng, lnb, *, tm, alpha):
    b, t, d = x.shape
    row = lambda bb, i: (bb, i, 0)
    const2 = lambda bb, i: (0, 0)
    const3 = lambda bb, i: (0, 0, 0)
    return pl.pallas_call(
        functools.partial(_merge_kernel, alpha=alpha),
        grid=(b, t // tm),
        in_specs=[pl.BlockSpec((1, tm, d), row),
                  pl.BlockSpec((1, 1, mod.shape[-1]), lambda bb, i: (bb, 0, 0)),
                  pl.BlockSpec((1, tm, W_BR), row),
                  pl.BlockSpec((1, tm, W_BR), row),
                  pl.BlockSpec((1, tm, W_BR), row),
                  pl.BlockSpec(wz.shape, const2),
                  pl.BlockSpec(wg.shape, const3),
                  pl.BlockSpec(bg.shape, const3),
                  pl.BlockSpec(wbr.shape, const3),
                  pl.BlockSpec(wo.shape, const2),
                  pl.BlockSpec(lng.shape, const2),
                  pl.BlockSpec(lnb.shape, const2)],
        out_specs=pl.BlockSpec((1, tm, d), row),
        out_shape=jax.ShapeDtypeStruct((b, t, d), F32),
        compiler_params=_cparams(2),
        name="merge",
    )(x, mod, ya, yb, yc, wz, wg, bg, wbr, wo, lng, lnb)


def _split_w_in(w):
    sizes = (W_BR, W_BR, W_BR, W_BR, W_BR, DKV_B, DKV_B, W_BR, H_IDX * D_IDX, D_IDX, H_IDX,
             W_BR, W_BR, W_BR, W_BR)
    cols, off = [], 0
    for s in sizes:
        cols.append(w[:, off:off + s])
        off += s
    aq, ak, av, az, bq, bk, bv, bz, biq, bik, biw, cq, ck, cv, cz = cols
    d = w.shape[0]
    w1 = jnp.concatenate([aq * ATT_SCALE, ak, av, cq * ATT_SCALE, ck, cv], axis=1).astype(BF16)
    w2 = jnp.concatenate([bk, bv, bik, jnp.zeros((d, N_W2 - 3 * DKV_B), w.dtype)],
                         axis=1).astype(BF16)
    wt = jnp.concatenate([bq * ATT_SCALE, biq * (D_IDX ** -0.5), bv, biw * IDX_HEAD_SCALE,
                          jnp.zeros((d, N_WT - ROW_BIW - H_IDX), w.dtype)], axis=1).T.astype(BF16)
    wz = jnp.concatenate([az, bz, cz], axis=1).astype(BF16)
    return w1, w2, wt, wz


def _key_blocks_t(v):
    b, t, n = v.shape
    return jnp.transpose(v.reshape(b, t // KEY_BLOCK, KEY_BLOCK, n), (0, 1, 3, 2))


def kernel(x_prompt, x_sample, c_prompt, c_sample, cache_a_k, cache_a_v, cache_b_k, cache_b_v,
           cache_b_kidx, cache_c_k, cache_c_v, w_ada, b_ada, w_in, w_gate, b_gate, w_branch,
           w_out, rel_bias, ln_g, ln_b):
    depth = w_in.shape[0]
    bp, seq, d = x_prompt.shape
    bs, dec_seq, _ = x_sample.shape
    past_len = cache_a_k.shape[2]
    band_rows = cache_c_k.shape[2]
    alpha = (2.0 * depth) ** 0.25
    assert seq % 512 == 0 and dec_seq <= CHUNK and past_len % DSA_TK == 0
    assert band_rows == BAND_ROWS and past_len // CHUNK == (past_len + dec_seq - 1) // CHUNK

    mod_all = _modulation(jnp.concatenate([c_prompt, c_sample], axis=0), w_ada, b_ada)
    bias_all = _band_bias(rel_bias)

    xp = x_prompt
    xs = jnp.pad(x_sample, ((0, 0), (0, SAMPLE_PAD - dec_seq), (0, 0)))
    n_sel_p = min(TOPK_MAX, seq // 4)
    n_sel_s = min(TOPK_MAX, (past_len + dec_seq) // 4)
    tm_p = 512
    st_p = [[] for _ in range(7)]
    st_s = [[] for _ in range(7)]

    for l in range(depth):
        w1, w2, wt, wz = _split_w_in(w_in[l])
        wg = w_gate[l].astype(BF16)
        bg = b_gate[l].reshape(N_BRANCH, 1, d)
        wbr = w_branch[l].astype(BF16)
        wo = w_out[l].astype(BF16)
        lng = ln_g[l].reshape(1, d)
        lnb = ln_b[l].reshape(1, d)
        mod_p = mod_all[l, :bp].reshape(bp, 1, 3 * d)
        mod_s = mod_all[l, bp:].reshape(bs, 1, 3 * d)
        bias_p = bias_all[l]

        (aq, ak, av, akb, avb, cq, ckb, cvb, ckt, cvt, bk, bv, bik, bkb, bikb,
         bqt, biqt, bvt, biwt) = _in_proj(xp, mod_p, w1, w2, wt, tm_p)
        ya = _stick(aq, akb, avb, akb, avb, tq=STICK_TQ, n_q_blocks=seq // STICK_TQ, n_main=None)
        yb = _dsa(biqt, biwt, bqt, bikb, bkb, bvt, n_q_blocks=seq // KEY_BLOCK, n_full=None,
                  q_off=0, n_valid=seq, n_sel=n_sel_p)
        yc = _band(cq, ckb, cvb, bias_p, tq=BAND_TQ, n_q_blocks=seq // BAND_TQ, sliding=True,
                   n_valid_win=BAND_W)
        xp = _merge(xp, mod_p, ya, yb, yc, wz, wg, bg, wbr, wo, lng, lnb, tm=256, alpha=alpha)
        for lst, val in zip(st_p, (ak.reshape(bp, seq, N_HEADS, D_HEAD),
                                   av.reshape(bp, seq, N_HEADS, D_HEAD), bk, bv, bik,
                                   ckt.reshape(bp, -1, N_HEADS, D_HEAD),
                                   cvt.reshape(bp, -1, N_HEADS, D_HEAD))):
            lst.append(val)

        (aq, ak, av, akb, avb, cq, ckb, cvb, ckt, cvt, bk, bv, bik, bkb, bikb,
         bqt, biqt, bvt, biwt) = _in_proj(xs, mod_s, w1, w2, wt, SAMPLE_PAD)
        ya = _stick(aq, akb, avb, cache_a_k[l].reshape(bs, past_len, W_BR),
                    cache_a_v[l].reshape(bs, past_len, W_BR),
                    tq=SAMPLE_PAD, n_q_blocks=1, n_main=past_len // KEY_BLOCK)
        key_pad = jnp.zeros((bs, DSA_TK - SAMPLE_PAD, DKV_B), BF16)
        ik_cat = jnp.concatenate([cache_b_kidx[l].astype(BF16), bikb, key_pad], axis=1)
        k_cat = jnp.concatenate([cache_b_k[l].astype(BF16), bkb, key_pad], axis=1)
        vt_cat = jnp.concatenate([_key_blocks_t(cache_b_v[l].astype(BF16)), bvt,
                                  _key_blocks_t(key_pad)], axis=1)
        yb = _dsa(biqt, biwt, bqt, ik_cat, k_cat, vt_cat, n_q_blocks=1,
                  n_full=past_len // DSA_TK, q_off=past_len, n_valid=past_len + dec_seq,
                  n_sel=n_sel_s)
        kc_cat = jnp.concatenate(
            [cache_c_k[l].reshape(bs, band_rows, W_BR).astype(BF16), ckb], axis=1)
        vc_cat = jnp.concatenate(
            [cache_c_v[l].reshape(bs, band_rows, W_BR).astype(BF16), cvb], axis=1)
        yc = _band(cq, kc_cat, vc_cat, bias_p[:, :SAMPLE_PAD, :band_rows + SAMPLE_PAD],
                   tq=SAMPLE_PAD, n_q_blocks=1, sliding=False, n_valid_win=band_rows + dec_seq)
        xs = _merge(xs, mod_s, ya, yb, yc, wz, wg, bg, wbr, wo, lng, lnb, tm=SAMPLE_PAD,
                    alpha=alpha)
        new_ck = jnp.concatenate(
            [cache_c_k[l], ckt[:, :dec_seq].reshape(bs, dec_seq, N_HEADS, D_HEAD)],
            axis=1)[:, -band_rows:]
        new_cv = jnp.concatenate(
            [cache_c_v[l], cvt[:, :dec_seq].reshape(bs, dec_seq, N_HEADS, D_HEAD)],
            axis=1)[:, -band_rows:]
        for lst, val in zip(st_s, (ak[:, :dec_seq].reshape(bs, dec_seq, N_HEADS, D_HEAD),
                                   av[:, :dec_seq].reshape(bs, dec_seq, N_HEADS, D_HEAD),
                                   bk[:, :dec_seq], bv[:, :dec_seq], bik[:, :dec_seq],
                                   new_ck, new_cv)):
            lst.append(val)

    return (xp, xs[:, :dec_seq], *[jnp.stack(v) for v in st_p], *[jnp.stack(v) for v in st_s])
```

```python
import functools

import jax
import jax.numpy as jnp
from jax import lax
from jax.experimental import pallas as pl
from jax.experimental.pallas import tpu as pltpu

F32 = jnp.float32
BF16 = jnp.bfloat16

CHUNK = 64
D_HEAD = 64
N_HEADS = 8
W_BR = N_HEADS * D_HEAD
DKV_B = 64
H_IDX = 4
D_IDX = 64
TOPK_MAX = 256
BAND_CHUNKS = 8
BAND_ROWS = BAND_CHUNKS * CHUNK
REL_CLIP = 128
N_BRANCH = 3
LN_EPS = 1e-5
NEG_INF = -1e30
ATT_SCALE = D_HEAD ** -0.5
IDX_HEAD_SCALE = H_IDX ** -0.5

LANE = 128
KEY_BLOCK = 128
SAMPLE_PAD = 128
VMEM_LIMIT = 56 * 1024 * 1024

KEY_LO = -2139095040
KEY_HI = 2139095040


def _cparams(n_axes):
    return pltpu.CompilerParams(
        dimension_semantics=("arbitrary",) * n_axes, vmem_limit_bytes=VMEM_LIMIT)


def _sigmoid(v):
    return 1.0 / (1.0 + jnp.exp(-v))


def _layer_norm(x):
    mu = jnp.mean(x, axis=-1, keepdims=True)
    xc = x - mu
    var = jnp.mean(xc * xc, axis=-1, keepdims=True)
    return xc * lax.rsqrt(var + LN_EPS)


def _dot(a, b):
    return jnp.dot(a, b, preferred_element_type=F32)


def _dot_nt(a, b):
    return lax.dot_general(a, b, (((1,), (1,)), ((), ())), preferred_element_type=F32)


def _mod_kernel(c_ref, w_ref, b_ref, o_ref):
    c = c_ref[...]
    s = c * _sigmoid(c)
    o_ref[0] = jnp.dot(s, w_ref[0], preferred_element_type=F32,
                       precision=lax.Precision.HIGHEST) + b_ref[0]


def _modulation(c_all, w_ada, b_ada):
    depth, d, d3 = w_ada.shape
    n = c_all.shape[0]
    tn = 1024
    return pl.pallas_call(
        _mod_kernel,
        grid=(depth, d3 // tn),
        in_specs=[pl.BlockSpec((n, d), lambda l, j: (0, 0)),
                  pl.BlockSpec((1, d, tn), lambda l, j: (l, 0, j)),
                  pl.BlockSpec((1, 1, tn), lambda l, j: (l, 0, j))],
        out_specs=pl.BlockSpec((1, n, tn), lambda l, j: (l, 0, j)),
        out_shape=jax.ShapeDtypeStruct((depth, n, d3), F32),
        compiler_params=_cparams(2),
        name="modulation",
    )(c_all, w_ada, b_ada.reshape(depth, 1, d3))


BAND_TQ = 256
BAND_W = BAND_ROWS + BAND_TQ
BIAS_ROWS_PER_STEP = 8
REL_PAD = 384


BAND_KEYS = BAND_ROWS + CHUNK
BAND_KEYS_PAD = 640
BAND_Q_CHUNKS = BAND_TQ // CHUNK


def _band_bias_kernel(tab_ref, o_ref):
    q0 = pl.program_id(1) * BIAS_ROWS_PER_STEP
    tab = tab_ref[0]
    col = lax.broadcasted_iota(jnp.int32, (1, BAND_KEYS_PAD), 1)
    sub = lax.broadcasted_iota(jnp.int32, (REL_PAD, BAND_KEYS_PAD), 0)
    nh = tab.shape[0]
    for rr in range(BIAS_ROWS_PER_STEP):
        idx = jnp.clip(q0 + rr - col + BAND_ROWS, -REL_CLIP, REL_CLIP) + REL_CLIP
        onehot = jnp.where(sub == idx, 1.0, 0.0).astype(F32)
        vals = jnp.dot(tab, onehot, preferred_element_type=F32,
                       precision=lax.Precision.HIGHEST)[:, :BAND_KEYS]
        for cc in range(BAND_Q_CHUNKS):
            left = jnp.full((nh, cc * CHUNK), NEG_INF, F32)
            right = jnp.full((nh, BAND_W - BAND_KEYS - cc * CHUNK), NEG_INF, F32)
            parts = [p for p in (left, vals, right) if p.shape[1] > 0]
            o_ref[0, cc, rr] = jnp.concatenate(parts, axis=1)


def _band_bias(rel_bias):
    depth, nh, nrel = rel_bias.shape
    tab = jnp.pad(rel_bias, ((0, 0), (0, 0), (0, REL_PAD - nrel)))
    out = pl.pallas_call(
        _band_bias_kernel,
        grid=(depth, CHUNK // BIAS_ROWS_PER_STEP),
        in_specs=[pl.BlockSpec((1, nh, REL_PAD), lambda l, i: (l, 0, 0))],
        out_specs=pl.BlockSpec((1, BAND_Q_CHUNKS, BIAS_ROWS_PER_STEP, nh, BAND_W),
                               lambda l, i: (l, 0, i, 0, 0)),
        out_shape=jax.ShapeDtypeStruct((depth, BAND_Q_CHUNKS, CHUNK, nh, BAND_W), F32),
        compiler_params=_cparams(2),
        name="band_bias",
    )(tab)
    out = out.reshape(depth, BAND_TQ, nh, BAND_W)
    return jnp.transpose(out, (0, 2, 1, 3))


N_W1 = 6 * W_BR
N_W2 = 256
N_WT = 840
ROW_BIQ = W_BR
ROW_BV = W_BR + H_IDX * D_IDX
ROW_BIW = ROW_BV + DKV_B


def _in_proj_kernel(x_ref, mod_ref, w1_ref, w2_ref, wt_ref, *refs, first_tail_step, n_aliased):
    (aq_ref, ak_ref, av_ref, akb_ref, avb_ref, cq_ref, ckb_ref, cvb_ref, ckt_ref, cvt_ref,
     bk_ref, bv_ref, bik_ref, bkb_ref, bikb_ref, bqt_ref, biqt_ref, bvt_ref,
     biwt_ref) = refs[n_aliased:]
    x = x_ref[0]
    d = x.shape[-1]
    tm = x.shape[0]
    mod = mod_ref[0]
    shift = mod[:, :d]
    scale = mod[:, d:2 * d]
    h = (_layer_norm(x) * (1.0 + scale) + shift).astype(BF16)

    def proj(j):
        return _dot(h, w1_ref[:, j * W_BR:(j + 1) * W_BR])

    aq_ref[0] = proj(0).astype(BF16)
    ak = proj(1)
    ak_ref[...] = ak.reshape(ak_ref.shape)
    akb_ref[0] = ak.astype(BF16)
    av = proj(2)
    av_ref[...] = av.reshape(av_ref.shape)
    avb_ref[0] = av.astype(BF16)
    cq_ref[0] = proj(3).astype(BF16)
    ck = proj(4)
    ckb_ref[0] = ck.astype(BF16)
    cv = proj(5)
    cvb_ref[0] = cv.astype(BF16)

    @pl.when(pl.program_id(1) >= first_tail_step)
    def _():
        ckt_ref[0] = ck
        cvt_ref[0] = cv

    small = _dot(h, w2_ref[...])
    bk = small[:, 0:DKV_B]
    bik = small[:, 2 * DKV_B:3 * DKV_B]
    bk_ref[0] = bk
    bv_ref[0] = small[:, DKV_B:2 * DKV_B]
    bik_ref[0] = bik
    bkb_ref[0] = bk.astype(BF16)
    bikb_ref[0] = bik.astype(BF16)

    tr = _dot_nt(wt_ref[...], h)
    bqt_ref[0] = tr[0:ROW_BIQ].astype(BF16)
    biqt_ref[0] = tr[ROW_BIQ:ROW_BV].astype(BF16)
    bvt = tr[ROW_BV:ROW_BIW].astype(BF16)
    for c in range(tm // KEY_BLOCK):
        bvt_ref[0, c] = bvt[:, c * KEY_BLOCK:(c + 1) * KEY_BLOCK]
    biwt_ref[0] = tr[ROW_BIW:N_WT]


def _in_proj(x, mod, w1, w2, wt, tm, layer_stack=None):
    b, t, d = x.shape
    nt = t // tm
    tail = min(BAND_ROWS, t)
    tail_blocks = tail // tm
    first_tail = nt - tail_blocks
    row = lambda bb, i: (bb, i, 0)
    colmaj = lambda bb, i: (bb, 0, i)
    tail_map = lambda bb, i: (bb, jnp.maximum(i - first_tail, 0), 0)

    def rows(n, dt):
        return pl.BlockSpec((1, tm, n), row), jax.ShapeDtypeStruct((b, t, n), dt)

    outs = [
        rows(W_BR, BF16),
        rows(W_BR, F32), rows(W_BR, F32),
        rows(W_BR, BF16), rows(W_BR, BF16),
        rows(W_BR, BF16), rows(W_BR, BF16), rows(W_BR, BF16),
        (pl.BlockSpec((1, tm, W_BR), tail_map), jax.ShapeDtypeStruct((b, tail, W_BR), F32)),
        (pl.BlockSpec((1, tm, W_BR), tail_map), jax.ShapeDtypeStruct((b, tail, W_BR), F32)),
        rows(DKV_B, F32), rows(DKV_B, F32), rows(D_IDX, F32),
        rows(DKV_B, BF16), rows(D_IDX, BF16),
        (pl.BlockSpec((1, W_BR, tm), colmaj), jax.ShapeDtypeStruct((b, W_BR, t), BF16)),
        (pl.BlockSpec((1, H_IDX * D_IDX, tm), colmaj),
         jax.ShapeDtypeStruct((b, H_IDX * D_IDX, t), BF16)),
        (pl.BlockSpec((1, tm // KEY_BLOCK, DKV_B, KEY_BLOCK), lambda bb, i: (bb, i, 0, 0)),
         jax.ShapeDtypeStruct((b, t // KEY_BLOCK, DKV_B, KEY_BLOCK), BF16)),
        (pl.BlockSpec((1, N_WT - ROW_BIW, tm), colmaj),
         jax.ShapeDtypeStruct((b, N_WT - ROW_BIW, t), F32)),
    ]
    aliased, aliases = [], {}
    if layer_stack is not None:
        layer, depth, ak_stack, av_stack = layer_stack
        slab = (pl.BlockSpec((1, 1, tm, W_BR), lambda bb, i: (layer, bb, i, 0)),
                jax.ShapeDtypeStruct((depth, b, t, W_BR), F32))
        outs[1] = outs[2] = slab
        if ak_stack is not None:
            aliased = [ak_stack, av_stack]
            aliases = {5: 1, 6: 2}
    return pl.pallas_call(
        functools.partial(_in_proj_kernel, first_tail_step=first_tail, n_aliased=len(aliased)),
        grid=(b, nt),
        in_specs=[pl.BlockSpec((1, tm, d), row),
                  pl.BlockSpec((1, 1, mod.shape[-1]), lambda bb, i: (bb, 0, 0)),
                  pl.BlockSpec((d, N_W1), lambda bb, i: (0, 0)),
                  pl.BlockSpec((d, N_W2), lambda bb, i: (0, 0)),
                  pl.BlockSpec((N_WT, d), lambda bb, i: (0, 0))]
        + [pl.BlockSpec(memory_space=pl.ANY)] * len(aliased),
        out_specs=[o[0] for o in outs],
        out_shape=[o[1] for o in outs],
        input_output_aliases=aliases,
        compiler_params=_cparams(2),
        name="in_proj",
    )(x, mod, w1, w2, wt, *aliased)


SOFTPLUS_CUT = 30.0
STICK_TQ = 256
STICK_RUN = 2


def _stick_kernel(q_ref, kd_ref, vd_ref, km_ref, vm_ref, o_ref, carry_ref, acc_ref, *, n_main):
    tq = q_ref.shape[1]
    tk = KEY_BLOCK
    n_diag = tq // tk
    qb = pl.program_id(1)
    n_blocks = qb * n_diag if n_main is None else n_main

    def suffix_matrix(n):
        jj = lax.broadcasted_iota(jnp.int32, (n, n), 0)
        ss = lax.broadcasted_iota(jnp.int32, (n, n), 1)
        return jnp.where(jj >= ss, 1.0, 0.0).astype(BF16)

    upper_of = {nb: suffix_matrix(nb * tk) for nb in sorted({n_diag, STICK_RUN})}
    qi = lax.broadcasted_iota(jnp.int32, (tq, tk), 0)
    kj = lax.broadcasted_iota(jnp.int32, (tq, tk), 1)

    carry_ref[...] = jnp.zeros(carry_ref.shape, F32)
    acc_ref[...] = jnp.zeros(acc_ref.shape, F32)

    heads = range(N_HEADS)
    pairs = range(N_HEADS // 2)
    pair_lanes = [slice(p * 2 * D_HEAD, (p + 1) * 2 * D_HEAD) for p in pairs]
    first_of_pair = lax.broadcasted_iota(jnp.int32, (tk, 2 * D_HEAD), 1) < D_HEAD

    def head_pair_diag(x):
        zero = jnp.zeros_like(x)
        return jnp.concatenate([jnp.where(first_of_pair, x, zero),
                                jnp.where(first_of_pair, zero, x)], axis=0)

    def step(kblks, vblks, visibles):
        nb = len(kblks)
        zs = [[None] * N_HEADS for _ in range(nb)]
        for b in range(nb):
            for p in pairs:
                z2 = _dot_nt(q_ref[0, :, pair_lanes[p]], head_pair_diag(kblks[b][:, pair_lanes[p]]))
                zs[b][2 * p], zs[b][2 * p + 1] = z2[:, :tk], z2[:, tk:]
        sps = [[None] * N_HEADS for _ in range(nb)]
        for b in range(nb):
            for h in heads:
                z = zs[b][h]
                sp = jnp.maximum(z, jnp.log(1.0 + jnp.exp(jnp.minimum(z, SOFTPLUS_CUT))))
                if visibles[b] is not None:
                    sp = jnp.where(visibles[b], sp, 0.0)
                sps[b][h] = sp.astype(BF16)
        order = list(reversed(range(nb)))
        upper = upper_of[nb]
        srs = [_dot(jnp.concatenate([sps[b][h] for b in order], axis=1), upper) for h in heads]
        run_visible = None
        if visibles[0] is not None:
            run_visible = jnp.concatenate([visibles[b] for b in order], axis=1)
        weights = []
        for h in heads:
            later = carry_ref[h]
            z_run = jnp.concatenate([zs[b][h] for b in order], axis=1)
            a = jnp.exp(z_run - srs[h] - jnp.concatenate([later] * nb, axis=1))
            if run_visible is not None:
                a = jnp.where(run_visible, a, 0.0)
            weights.append(a.astype(BF16))
            carry_ref[h] = later + jnp.broadcast_to(srs[h][:, 0:1], (tq, tk))
        for p in pairs:
            w2 = jnp.concatenate([weights[2 * p], weights[2 * p + 1]], axis=1)
            vd = [head_pair_diag(vblks[b][:, pair_lanes[p]]) for b in order]
            v2 = jnp.concatenate([v[:tk] for v in vd] + [v[tk:] for v in vd], axis=0)
            acc_ref[p] += _dot(w2, v2)

    def diag_rows(dd):
        return slice(dd * tk, (dd + 1) * tk)

    diag = list(reversed(range(n_diag)))
    step([kd_ref[0, diag_rows(dd), :].astype(BF16) for dd in diag],
         [vd_ref[0, diag_rows(dd), :].astype(BF16) for dd in diag],
         [dd * tk + kj < qi for dd in diag])

    def key_rows(blk):
        return pl.ds(pl.multiple_of(blk * tk, tk), tk)

    def body(i, c):
        newest = n_blocks - 1 - STICK_RUN * i
        blks = [newest - j for j in range(STICK_RUN)]
        step([km_ref[0, key_rows(blk), :].astype(BF16) for blk in blks],
             [vm_ref[0, key_rows(blk), :].astype(BF16) for blk in blks], [None] * STICK_RUN)
        return c

    lax.fori_loop(0, n_blocks // STICK_RUN, body, 0)
    o_ref[0] = jnp.concatenate([acc_ref[p] for p in pairs], axis=1)


def _stick(q, k_diag, v_diag, k_main, v_main, *, tq, n_q_blocks, n_main):
    b = q.shape[0]
    tm = k_main.shape[1]
    blk = lambda bb, i: (bb, i, 0)
    whole = lambda bb, i: (bb, 0, 0)
    return pl.pallas_call(
        functools.partial(_stick_kernel, n_main=n_main),
        grid=(b, n_q_blocks),
        in_specs=[pl.BlockSpec((1, tq, W_BR), blk),
                  pl.BlockSpec((1, tq, W_BR), blk),
                  pl.BlockSpec((1, tq, W_BR), blk),
                  pl.BlockSpec((1, tm, W_BR), whole),
                  pl.BlockSpec((1, tm, W_BR), whole)],
        out_specs=pl.BlockSpec((1, tq, W_BR), blk),
        out_shape=jax.ShapeDtypeStruct((b, n_q_blocks * tq, W_BR), F32),
        scratch_shapes=[pltpu.VMEM((N_HEADS, tq, KEY_BLOCK), F32),
                        pltpu.VMEM((N_HEADS // 2, tq, 2 * D_HEAD), F32)],
        compiler_params=_cparams(2),
        name="stick_breaking",
    )(q, k_diag, v_diag, k_main, v_main)


def _flip_magnitude_if_negative(word):
    return word ^ (lax.shift_right_arithmetic(word, 31) & jnp.int32(0x7FFFFFFF))


def _key_to_float(key):
    return lax.bitcast_convert_type(_flip_magnitude_if_negative(key), F32)


def _float_to_key(x):
    return _flip_magnitude_if_negative(lax.bitcast_convert_type(x, jnp.int32))


DSA_TK = 2 * KEY_BLOCK
ALIBI_ROWS = 64
BISECT_GROUP = 4
BISECT_FREE_GROUPS = 3


def _dsa_kernel(biqt_ref, biwt_ref, bqt_ref, ik_ref, k_ref, vt_ref, o_ref, sc_ref,
                *, n_full, q_off, n_valid, n_sel):
    tq = bqt_ref.shape[2]
    tk = DSA_TK
    sub = tk // KEY_BLOCK
    qb = pl.program_id(1)
    n_full_blocks = (qb * tq) // tk if n_full is None else n_full
    n_blocks = n_full_blocks + 1
    iw = biwt_ref[0]
    q0 = q_off + qb * tq

    kloc = lax.broadcasted_iota(jnp.int32, (tk, tq), 0)
    qpos = q0 + lax.broadcasted_iota(jnp.int32, (tk, tq), 1)

    def score_block(kb):
        start = pl.multiple_of(kb * tk, tk)
        ikb = ik_ref[0, pl.ds(start, tk), :].astype(BF16)
        sc = jnp.zeros((tk, tq), F32)
        for hp in range(H_IDX // 2):
            iq2 = jnp.concatenate(
                [biqt_ref[0, h * D_IDX:(h + 1) * D_IDX, :] for h in (2 * hp, 2 * hp + 1)], axis=1)
            lg2 = _dot(ikb, iq2)
            for u in range(2):
                h = 2 * hp + u
                sc = sc + iw[h:h + 1, :] * jnp.maximum(lg2[:, u * tq:(u + 1) * tq], 0.0)
        return jnp.where(sc == 0.0, 0.0, sc)

    def fill(j, c):
        kbs = [jnp.minimum(2 * j + u, n_full_blocks - 1) for u in range(2)]
        scores = [score_block(kb) for kb in kbs]
        for kb, sc in zip(kbs, scores):
            sc_ref[kb] = sc
        return c

    lax.fori_loop(0, (n_full_blocks + 1) // 2, fill, 0)
    kpos_last = n_full_blocks * tk + kloc
    admissible = ((lax.shift_right_logical(kpos_last, 6) <= lax.shift_right_logical(qpos, 6))
                  & (kpos_last < n_valid))
    sc_ref[n_full_blocks] = jnp.where(admissible, score_block(n_full_blocks), -jnp.inf)
    sc_ref[n_blocks] = jnp.full((tk, tq), -jnp.inf, F32)
    n_pairs = (n_blocks + 1) // 2

    def count(pred):
        def body(j, acc):
            for u in range(2):
                for r in range(sub):
                    x = sc_ref[2 * j + u, r * KEY_BLOCK:(r + 1) * KEY_BLOCK, :]
                    acc = acc + jnp.where(pred(x), 1.0, 0.0)
            return acc
        acc = lax.fori_loop(0, n_pairs, body, jnp.zeros((KEY_BLOCK, tq), F32))
        return jnp.sum(acc, axis=0, keepdims=True)

    k_sel = jnp.float32(n_sel)

    n_ge0 = count(lambda x: x >= 0.0)
    n_gt0 = count(lambda x: x > 0.0)
    zero_thr = (n_ge0 >= k_sel) & (n_gt0 < k_sel)
    positive = n_gt0 >= k_sel
    assert tk >= n_sel
    gmax = lax.fori_loop(0, n_blocks, lambda kb, g: jnp.maximum(g, sc_ref[kb]),
                         jnp.full((tk, tq), -jnp.inf, F32))
    lo_b = jnp.maximum(_float_to_key(jnp.min(gmax, axis=0, keepdims=True)), KEY_LO)
    hi_b = jnp.minimum(_float_to_key(jnp.max(gmax, axis=0, keepdims=True)), KEY_HI - 1) + 1
    lo0 = jnp.where(zero_thr, 0, jnp.where(positive, jnp.maximum(lo_b, 1), lo_b))
    hi0 = jnp.where(zero_thr, 1, jnp.where(positive, hi_b, jnp.minimum(hi_b, 0)))

    def unresolved(lo, hi):
        return (hi - lo) != 1

    def bisect_cond(st):
        it, lo, hi = st
        return ((it < 32 // BISECT_GROUP)
                & (jnp.max(jnp.where(unresolved(lo, hi), 1.0, 0.0)) > 0.5))

    def bisect(st):
        it, lo, hi = st
        for _ in range(BISECT_GROUP):
            active = unresolved(lo, hi)
            mid = lo + lax.shift_right_logical(hi - lo, 1)
            n_ge = count(lambda x, mid=mid: x >= _key_to_float(mid))
            ge = n_ge >= k_sel
            exact = n_ge == k_sel
            lo, hi = (jnp.where(active & ge, mid, lo),
                      jnp.where(active, jnp.where(exact, mid + 1, jnp.where(ge, hi, mid)), hi))
        return it + 1, lo, hi

    st = lax.fori_loop(0, BISECT_FREE_GROUPS, lambda _, s: bisect(s), (jnp.int32(0), lo0, hi0))
    _, lo, _ = lax.while_loop(bisect_cond, bisect, st)
    thr = _key_to_float(lo)
    need = k_sel - count(lambda x: x > thr)

    srow = lax.broadcasted_iota(jnp.int32, (tk, tk), 0)
    jcol = lax.broadcasted_iota(jnp.int32, (tk, tk), 1)
    before = jnp.where(jcol < srow, 1.0, 0.0).astype(BF16)

    def select(j, seen):
        xs = [sc_ref[2 * j + u] for u in range(2)]
        eqs = [x == thr for x in xs]
        eqfs = [jnp.where(eq, 1.0, 0.0) for eq in eqs]
        within = [_dot(before, eqf.astype(BF16)) for eqf in eqfs]
        for u in range(2):
            chosen = (xs[u] > thr) | (eqs[u] & (within[u] + seen < need))
            sc_ref[2 * j + u] = jnp.where(chosen, 0.0, NEG_INF)
            seen = seen + jnp.sum(eqfs[u], axis=0, keepdims=True)
        return seen

    lax.fori_loop(0, n_pairs, select, jnp.zeros((1, tq), F32))

    q_all = jnp.concatenate(
        [bqt_ref[0, h * D_HEAD:(h + 1) * D_HEAD, :] for h in range(N_HEADS)], axis=1)
    slope = jnp.concatenate(
        [jnp.full((1, tq), 2.0 ** -(h + 1), F32) for h in range(N_HEADS)], axis=1)
    qi_all = jnp.concatenate(
        [lax.broadcasted_iota(jnp.int32, (1, tq), 1)] * N_HEADS, axis=1).astype(F32)
    arow = lax.broadcasted_iota(jnp.int32, (ALIBI_ROWS, N_HEADS * tq), 0)
    q_extra = jnp.where(arow == 0, -slope * qi_all, jnp.where(arow <= 2, slope, 0.0))
    q_aug = jnp.concatenate([q_all, q_extra.astype(BF16)], axis=0)
    acol = lax.broadcasted_iota(jnp.int32, (tk, ALIBI_ROWS), 1)
    kj_f = lax.broadcasted_iota(jnp.int32, (tk, ALIBI_ROWS), 0).astype(F32)

    def load_kv(kb):
        start = pl.multiple_of(kb * tk, tk)
        kblk = k_ref[0, pl.ds(start, tk), :].astype(BF16)
        vblk_t = jnp.concatenate(
            [vt_ref[0, kb * sub + j].astype(BF16) for j in range(sub)], axis=1)
        return kblk, vblk_t

    def softmax_step(st, s_heads, vblk_t):
        ms, ls, accs = st
        m_new, new_l, new_acc = [], [], []
        for hp in range(N_HEADS // 2):
            hs = (2 * hp, 2 * hp + 1)
            m2 = [jnp.maximum(ms[h], jnp.max(s_heads[h], axis=0, keepdims=True)) for h in hs]
            p2 = [jnp.exp(s_heads[h] - m2[u]) for u, h in enumerate(hs)]
            pv2 = _dot(vblk_t, jnp.concatenate([p.astype(BF16) for p in p2], axis=1))
            for u, h in enumerate(hs):
                alpha = jnp.exp(ms[h] - m2[u])
                m_new.append(m2[u])
                new_l.append(alpha * ls[h] + jnp.sum(p2[u], axis=0, keepdims=True))
                new_acc.append(alpha * accs[h] + pv2[:, u * tq:(u + 1) * tq])
        return tuple(m_new), tuple(new_l), tuple(new_acc)

    def per_head_logits(keys, queries):
        tiles = []
        for hp in range(N_HEADS // 2):
            s2 = _dot(keys, queries[:, 2 * hp * tq:(2 * hp + 2) * tq])
            tiles += [s2[:, :tq], s2[:, tq:]]
        return tiles

    def full_block_logits(kb, mb):
        kblk, vblk_t = load_kv(kb)
        neg_delta = (kb * tk - q0).astype(F32)
        k_extra = jnp.where(acol == 0, 1.0,
                            jnp.where(acol == 1, kj_f, jnp.where(acol == 2, neg_delta, 0.0)))
        k_aug = jnp.concatenate([kblk, k_extra.astype(BF16)], axis=1)
        return [s + mb for s in per_head_logits(k_aug, q_aug)], vblk_t

    def attend_pair(parts, st):
        s_heads = [jnp.concatenate([parts[0][0][h], parts[1][0][h]], axis=0)
                   for h in range(N_HEADS)]
        return softmax_step(st, s_heads, jnp.concatenate([parts[0][1], parts[1][1]], axis=1))

    def attend_full_pair(j, st):
        return attend_pair([full_block_logits(2 * j + u, sc_ref[2 * j + u]) for u in range(2)], st)

    st = lax.fori_loop(
        0, n_full_blocks // 2, attend_full_pair,
        (tuple(jnp.full((1, tq), -3e38, F32) for _ in range(N_HEADS)),
         tuple(jnp.zeros((1, tq), F32) for _ in range(N_HEADS)),
         tuple(jnp.zeros((DKV_B, tq), F32) for _ in range(N_HEADS))))

    odd = (n_full_blocks % 2) == 1
    spare = jnp.maximum(n_full_blocks - 1, 0)
    spare_part = full_block_logits(spare, jnp.where(odd, sc_ref[spare], NEG_INF))
    kblk, vblk_t = load_kv(n_full_blocks)
    dist = jnp.abs(qpos - kpos_last).astype(F32)
    mb = sc_ref[n_full_blocks]
    last_part = ([s + (mb - (2.0 ** -(h + 1)) * dist)
                  for h, s in enumerate(per_head_logits(kblk, q_all))], vblk_t)
    _, ls, accs = attend_pair([spare_part, last_part], st)

    o_ref[0] = jnp.concatenate([(accs[h] / ls[h]).T for h in range(N_HEADS)], axis=1)


def _dsa(biqt, biwt, bqt, ik, k, vt, *, n_q_blocks, n_full, q_off, n_valid, n_sel):
    b = bqt.shape[0]
    tq = KEY_BLOCK
    tk_total = ik.shape[1]
    colmaj = lambda bb, i: (bb, 0, i)
    whole3 = lambda bb, i: (bb, 0, 0)
    return pl.pallas_call(
        functools.partial(_dsa_kernel, n_full=n_full, q_off=q_off, n_valid=n_valid, n_sel=n_sel),
        grid=(b, n_q_blocks),
        in_specs=[pl.BlockSpec((1, H_IDX * D_IDX, tq), colmaj),
                  pl.BlockSpec((1, biwt.shape[1], tq), colmaj),
                  pl.BlockSpec((1, W_BR, tq), colmaj),
                  pl.BlockSpec((1, tk_total, D_IDX), whole3),
                  pl.BlockSpec((1, tk_total, DKV_B), whole3),
                  pl.BlockSpec((1, tk_total // KEY_BLOCK, DKV_B, KEY_BLOCK),
                               lambda bb, i: (bb, 0, 0, 0))],
        out_specs=pl.BlockSpec((1, tq, W_BR), lambda bb, i: (bb, i, 0)),
        out_shape=jax.ShapeDtypeStruct((b, n_q_blocks * tq, W_BR), F32),
        scratch_shapes=[pltpu.VMEM((tk_total // DSA_TK + 1, DSA_TK, tq), F32)],
        compiler_params=_cparams(2),
        name="dsa",
    )(biqt, biwt, bqt, ik, k, vt)


def _band_kernel(*refs, n_win, win_start_blocks, n_valid_win):
    q_ref = refs[0]
    k_refs = refs[1:1 + n_win]
    v_refs = refs[1 + n_win:1 + 2 * n_win]
    bias_ref = refs[1 + 2 * n_win]
    o_ref = refs[2 + 2 * n_win]
    tq = q_ref.shape[1]
    w = n_win * tq
    i = pl.program_id(1)
    col = lax.broadcasted_iota(jnp.int32, (1, w), 1)
    if win_start_blocks is None:
        valid = col < n_valid_win
    else:
        valid = ((i + win_start_blocks) * tq + col >= 0) & (col < n_valid_win)
    kwin = jnp.concatenate([r[0] for r in k_refs], axis=0)
    vwin = jnp.concatenate([r[0] for r in v_refs], axis=0)
    outs = []
    for h in range(N_HEADS):
        lanes = slice(h * D_HEAD, (h + 1) * D_HEAD)
        s = _dot_nt(q_ref[0, :, lanes], kwin[:, lanes]) + bias_ref[h]
        s = jnp.where(valid, s, NEG_INF)
        e = jnp.exp(s - jnp.max(s, axis=-1, keepdims=True))
        den = jnp.sum(e, axis=-1, keepdims=True)
        outs.append(_dot(e.astype(BF16), vwin[:, lanes]) / den)
    o_ref[0] = jnp.concatenate(outs, axis=1)


def _band(q, k, v, bias, *, tq, n_q_blocks, sliding, n_valid_win):
    b = q.shape[0]
    n_win = bias.shape[2] // tq
    assert bias.shape[1] == tq

    def kv_map(wi):
        if sliding:
            return lambda bb, i: (bb, jnp.maximum(i - (n_win - 1) + wi, 0), 0)
        return lambda bb, i: (bb, wi, 0)

    kv_specs = [pl.BlockSpec((1, tq, W_BR), kv_map(wi)) for wi in range(n_win)]
    return pl.pallas_call(
        functools.partial(_band_kernel, n_win=n_win,
                          win_start_blocks=-(n_win - 1) if sliding else None,
                          n_valid_win=n_valid_win),
        grid=(b, n_q_blocks),
        in_specs=([pl.BlockSpec((1, tq, W_BR), lambda bb, i: (bb, i, 0))] + kv_specs + kv_specs
                  + [pl.BlockSpec(bias.shape, lambda bb, i: (0, 0, 0))]),
        out_specs=pl.BlockSpec((1, tq, W_BR), lambda bb, i: (bb, i, 0)),
        out_shape=jax.ShapeDtypeStruct((b, n_q_blocks * tq, W_BR), F32),
        compiler_params=_cparams(2),
        name="band",
    )(q, *([k] * n_win), *([v] * n_win), bias)


def _merge_kernel(x_ref, mod_ref, ya_ref, yb_ref, yc_ref, wz_ref, wg_ref, bg_ref, wbr_ref,
                  wo_ref, lng_ref, lnb_ref, o_ref, *, alpha):
    x = x_ref[0]
    d = x.shape[-1]
    mod = mod_ref[0]
    shift = mod[:, :d]
    scale = mod[:, d:2 * d]
    gate = mod[:, 2 * d:3 * d]
    h = (_layer_norm(x) * (1.0 + scale) + shift).astype(BF16)
    m = jnp.zeros(x.shape, F32)
    for n, y_ref in enumerate((ya_ref, yb_ref, yc_ref)):
        z = _dot(h, wz_ref[:, n * W_BR:(n + 1) * W_BR])
        u = (y_ref[0] * (z * _sigmoid(z))).astype(BF16)
        t = _dot(u, wbr_ref[n])
        g = _sigmoid(_dot(h, wg_ref[n]) + bg_ref[n])
        m = m + g * t
    out = _dot(m.astype(BF16), wo_ref[...])
    r = alpha * x + gate * out
    o_ref[0] = _layer_norm(r) * lng_ref[...] + lnb_ref[...]


def _merge(x, mod, ya, yb, yc, wz, wg, bg, wbr, wo, lng, lnb, *, tm, alpha):
    b, t, d = x.shape
    row = lambda bb, i: (bb, i, 0)
    const2 = lambda bb, i: (0, 0)
    const3 = lambda bb, i: (0, 0, 0)
    return pl.pallas_call(
        functools.partial(_merge_kernel, alpha=alpha),
        grid=(b, t // tm),
        in_specs=[pl.BlockSpec((1, tm, d), row),
                  pl.BlockSpec((1, 1, mod.shape[-1]), lambda bb, i: (bb, 0, 0)),
                  pl.BlockSpec((1, tm, W_BR), row),
                  pl.BlockSpec((1, tm, W_BR), row),
                  pl.BlockSpec((1, tm, W_BR), row),
                  pl.BlockSpec(wz.shape, const2),
                  pl.BlockSpec(wg.shape, const3),
                  pl.BlockSpec(bg.shape, const3),
                  pl.BlockSpec(wbr.shape, const3),
                  pl.BlockSpec(wo.shape, const2),
                  pl.BlockSpec(lng.shape, const2),
                  pl.BlockSpec(lnb.shape, const2)],
        out_specs=pl.BlockSpec((1, tm, d), row),
        out_shape=jax.ShapeDtypeStruct((b, t, d), F32),
        compiler_params=_cparams(2),
        name="merge",
    )(x, mod, ya, yb, yc, wz, wg, bg, wbr, wo, lng, lnb)


def _split_w_in(w):
    sizes = (W_BR, W_BR, W_BR, W_BR, W_BR, DKV_B, DKV_B, W_BR, H_IDX * D_IDX, D_IDX, H_IDX,
             W_BR, W_BR, W_BR, W_BR)
    cols, off = [], 0
    for s in sizes:
        cols.append(w[:, off:off + s])
        off += s
    aq, ak, av, az, bq, bk, bv, bz, biq, bik, biw, cq, ck, cv, cz = cols
    d = w.shape[0]
    w1 = jnp.concatenate([aq * ATT_SCALE, ak, av, cq * ATT_SCALE, ck, cv], axis=1).astype(BF16)
    w2 = jnp.concatenate([bk, bv, bik, jnp.zeros((d, N_W2 - 3 * DKV_B), w.dtype)],
                         axis=1).astype(BF16)
    wt = jnp.concatenate([bq * ATT_SCALE, biq * (D_IDX ** -0.5), bv, biw * IDX_HEAD_SCALE,
                          jnp.zeros((d, N_WT - ROW_BIW - H_IDX), w.dtype)], axis=1).T.astype(BF16)
    wz = jnp.concatenate([az, bz, cz], axis=1).astype(BF16)
    return w1, w2, wt, wz


def _key_blocks_t(v):
    b, t, n = v.shape
    return jnp.transpose(v.reshape(b, t // KEY_BLOCK, KEY_BLOCK, n), (0, 1, 3, 2))


def kernel(x_prompt, x_sample, c_prompt, c_sample, cache_a_k, cache_a_v, cache_b_k, cache_b_v,
           cache_b_kidx, cache_c_k, cache_c_v, w_ada, b_ada, w_in, w_gate, b_gate, w_branch,
           w_out, rel_bias, ln_g, ln_b):
    depth = w_in.shape[0]
    bp, seq, d = x_prompt.shape
    bs, dec_seq, _ = x_sample.shape
    past_len = cache_a_k.shape[2]
    band_rows = cache_c_k.shape[2]
    alpha = (2.0 * depth) ** 0.25
    assert seq % 512 == 0 and dec_seq <= CHUNK and past_len % DSA_TK == 0
    assert band_rows == BAND_ROWS and past_len // CHUNK == (past_len + dec_seq - 1) // CHUNK

    mod_all = _modulation(jnp.concatenate([c_prompt, c_sample], axis=0), w_ada, b_ada)
    bias_all = _band_bias(rel_bias)

    xp = x_prompt
    xs = jnp.pad(x_sample, ((0, 0), (0, SAMPLE_PAD - dec_seq), (0, 0)))
    n_sel_p = min(TOPK_MAX, seq // 4)
    n_sel_s = min(TOPK_MAX, (past_len + dec_seq) // 4)
    tm_p = 512
    st_p = [[] for _ in range(5)]
    st_s = [[] for _ in range(7)]
    ak_stack = av_stack = None

    for l in range(depth):
        w1, w2, wt, wz = _split_w_in(w_in[l])
        wg = w_gate[l].astype(BF16)
        bg = b_gate[l].reshape(N_BRANCH, 1, d)
        wbr = w_branch[l].astype(BF16)
        wo = w_out[l].astype(BF16)
        lng = ln_g[l].reshape(1, d)
        lnb = ln_b[l].reshape(1, d)
        mod_p = mod_all[l, :bp].reshape(bp, 1, 3 * d)
        mod_s = mod_all[l, bp:].reshape(bs, 1, 3 * d)
        bias_p = bias_all[l]

        (aq, ak_stack, av_stack, akb, avb, cq, ckb, cvb, ckt, cvt, bk, bv, bik, bkb, bikb,
         bqt, biqt, bvt, biwt) = _in_proj(xp, mod_p, w1, w2, wt, tm_p,
                                          layer_stack=(l, depth, ak_stack, av_stack))
        ya = _stick(aq, akb, avb, akb, avb, tq=STICK_TQ, n_q_blocks=seq // STICK_TQ, n_main=None)
        yb = _dsa(biqt, biwt, bqt, bikb, bkb, bvt, n_q_blocks=seq // KEY_BLOCK, n_full=None,
                  q_off=0, n_valid=seq, n_sel=n_sel_p)
        yc = _band(cq, ckb, cvb, bias_p, tq=BAND_TQ, n_q_blocks=seq // BAND_TQ, sliding=True,
                   n_valid_win=BAND_W)
        xp = _merge(xp, mod_p, ya, yb, yc, wz, wg, bg, wbr, wo, lng, lnb, tm=512, alpha=alpha)
        for lst, val in zip(st_p, (bk, bv, bik, ckt.reshape(bp, -1, N_HEADS, D_HEAD),
                                   cvt.reshape(bp, -1, N_HEADS, D_HEAD))):
            lst.append(val)

        (aq, ak, av, akb, avb, cq, ckb, cvb, ckt, cvt, bk, bv, bik, bkb, bikb,
         bqt, biqt, bvt, biwt) = _in_proj(xs, mod_s, w1, w2, wt, SAMPLE_PAD)
        ya = _stick(aq, akb, avb, cache_a_k[l].reshape(bs, past_len, W_BR),
                    cache_a_v[l].reshape(bs, past_len, W_BR),
                    tq=SAMPLE_PAD, n_q_blocks=1, n_main=past_len // KEY_BLOCK)
        key_pad = jnp.zeros((bs, DSA_TK - SAMPLE_PAD, DKV_B), BF16)
        ik_cat = jnp.concatenate([cache_b_kidx[l].astype(BF16), bikb, key_pad], axis=1)
        k_cat = jnp.concatenate([cache_b_k[l].astype(BF16), bkb, key_pad], axis=1)
        vt_cat = jnp.concatenate([_key_blocks_t(cache_b_v[l].astype(BF16)), bvt,
                                  _key_blocks_t(key_pad)], axis=1)
        yb = _dsa(biqt, biwt, bqt, ik_cat, k_cat, vt_cat, n_q_blocks=1,
                  n_full=past_len // DSA_TK, q_off=past_len, n_valid=past_len + dec_seq,
                  n_sel=n_sel_s)
        kc_cat = jnp.concatenate(
            [cache_c_k[l].reshape(bs, band_rows, W_BR).astype(BF16), ckb], axis=1)
        vc_cat = jnp.concatenate(
            [cache_c_v[l].reshape(bs, band_rows, W_BR).astype(BF16), cvb], axis=1)
        yc = _band(cq, kc_cat, vc_cat, bias_p[:, :SAMPLE_PAD, :band_rows + SAMPLE_PAD],
                   tq=SAMPLE_PAD, n_q_blocks=1, sliding=False, n_valid_win=band_rows + dec_seq)
        xs = _merge(xs, mod_s, ya, yb, yc, wz, wg, bg, wbr, wo, lng, lnb, tm=SAMPLE_PAD,
                    alpha=alpha)
        new_ck = jnp.concatenate(
            [cache_c_k[l], ckt[:, :dec_seq].reshape(bs, dec_seq, N_HEADS, D_HEAD)],
            axis=1)[:, -band_rows:]
        new_cv = jnp.concatenate(
            [cache_c_v[l], cvt[:, :dec_seq].reshape(bs, dec_seq, N_HEADS, D_HEAD)],
            axis=1)[:, -band_rows:]
        for lst, val in zip(st_s, (ak[:, :dec_seq].reshape(bs, dec_seq, N_HEADS, D_HEAD),
                                   av[:, :dec_seq].reshape(bs, dec_seq, N_HEADS, D_HEAD),
                                   bk[:, :dec_seq], bv[:, :dec_seq], bik[:, :dec_seq],
                                   new_ck, new_cv)):
            lst.append(val)

    return (xp, xs[:, :dec_seq],
            ak_stack.reshape(depth, bp, seq, N_HEADS, D_HEAD),
            av_stack.reshape(depth, bp, seq, N_HEADS, D_HEAD),
            *[jnp.stack(v) for v in st_p], *[jnp.stack(v) for v in st_s])
```

```python
import functools

import jax
import jax.numpy as jnp
from jax import lax
from jax.experimental import pallas as pl
from jax.experimental.pallas import tpu as pltpu

F32 = jnp.float32
BF16 = jnp.bfloat16

CHUNK = 64
D_HEAD = 64
N_HEADS = 8
W_BR = N_HEADS * D_HEAD
DKV_B = 64
H_IDX = 4
D_IDX = 64
TOPK_MAX = 256
BAND_CHUNKS = 8
BAND_ROWS = BAND_CHUNKS * CHUNK
REL_CLIP = 128
N_BRANCH = 3
LN_EPS = 1e-5
NEG_INF = -1e30
ATT_SCALE = D_HEAD ** -0.5
IDX_HEAD_SCALE = H_IDX ** -0.5

LANE = 128
KEY_BLOCK = 128
SAMPLE_PAD = 128
VMEM_LIMIT = 56 * 1024 * 1024

KEY_LO = -2139095040
KEY_HI = 2139095040


def _cparams(n_axes):
    return pltpu.CompilerParams(
        dimension_semantics=("arbitrary",) * n_axes, vmem_limit_bytes=VMEM_LIMIT)


def _sigmoid(v):
    return 1.0 / (1.0 + jnp.exp(-v))


def _layer_norm(x):
    mu = jnp.mean(x, axis=-1, keepdims=True)
    xc = x - mu
    var = jnp.mean(xc * xc, axis=-1, keepdims=True)
    return xc * lax.rsqrt(var + LN_EPS)


def _dot(a, b):
    return jnp.dot(a, b, preferred_element_type=F32)


def _dot_nt(a, b):
    return lax.dot_general(a, b, (((1,), (1,)), ((), ())), preferred_element_type=F32)


def _mod_kernel(c_ref, w_ref, b_ref, o_ref):
    c = c_ref[...]
    s = c * _sigmoid(c)
    o_ref[0] = jnp.dot(s, w_ref[0], preferred_element_type=F32,
                       precision=lax.Precision.HIGHEST) + b_ref[0]


def _modulation(c_all, w_ada, b_ada):
    depth, d, d3 = w_ada.shape
    n = c_all.shape[0]
    tn = 1024
    return pl.pallas_call(
        _mod_kernel,
        grid=(depth, d3 // tn),
        in_specs=[pl.BlockSpec((n, d), lambda l, j: (0, 0)),
                  pl.BlockSpec((1, d, tn), lambda l, j: (l, 0, j)),
                  pl.BlockSpec((1, 1, tn), lambda l, j: (l, 0, j))],
        out_specs=pl.BlockSpec((1, n, tn), lambda l, j: (l, 0, j)),
        out_shape=jax.ShapeDtypeStruct((depth, n, d3), F32),
        compiler_params=_cparams(2),
        name="modulation",
    )(c_all, w_ada, b_ada.reshape(depth, 1, d3))


BAND_TQ = 256
BAND_W = BAND_ROWS + BAND_TQ
BIAS_ROWS_PER_STEP = 8
REL_PAD = 384


BAND_KEYS = BAND_ROWS + CHUNK
BAND_KEYS_PAD = 640
BAND_Q_CHUNKS = BAND_TQ // CHUNK


def _band_bias_kernel(tab_ref, o_ref):
    q0 = pl.program_id(1) * BIAS_ROWS_PER_STEP
    tab = tab_ref[0]
    col = lax.broadcasted_iota(jnp.int32, (1, BAND_KEYS_PAD), 1)
    sub = lax.broadcasted_iota(jnp.int32, (REL_PAD, BAND_KEYS_PAD), 0)
    nh = tab.shape[0]
    for rr in range(BIAS_ROWS_PER_STEP):
        idx = jnp.clip(q0 + rr - col + BAND_ROWS, -REL_CLIP, REL_CLIP) + REL_CLIP
        onehot = jnp.where(sub == idx, 1.0, 0.0).astype(F32)
        vals = jnp.dot(tab, onehot, preferred_element_type=F32,
                       precision=lax.Precision.HIGHEST)[:, :BAND_KEYS]
        for cc in range(BAND_Q_CHUNKS):
            left = jnp.full((nh, cc * CHUNK), NEG_INF, F32)
            right = jnp.full((nh, BAND_W - BAND_KEYS - cc * CHUNK), NEG_INF, F32)
            parts = [p for p in (left, vals, right) if p.shape[1] > 0]
            o_ref[0, cc, rr] = jnp.concatenate(parts, axis=1)


def _band_bias(rel_bias):
    depth, nh, nrel = rel_bias.shape
    tab = jnp.pad(rel_bias, ((0, 0), (0, 0), (0, REL_PAD - nrel)))
    out = pl.pallas_call(
        _band_bias_kernel,
        grid=(depth, CHUNK // BIAS_ROWS_PER_STEP),
        in_specs=[pl.BlockSpec((1, nh, REL_PAD), lambda l, i: (l, 0, 0))],
        out_specs=pl.BlockSpec((1, BAND_Q_CHUNKS, BIAS_ROWS_PER_STEP, nh, BAND_W),
                               lambda l, i: (l, 0, i, 0, 0)),
        out_shape=jax.ShapeDtypeStruct((depth, BAND_Q_CHUNKS, CHUNK, nh, BAND_W), F32),
        compiler_params=_cparams(2),
        name="band_bias",
    )(tab)
    out = out.reshape(depth, BAND_TQ, nh, BAND_W)
    return jnp.transpose(out, (0, 2, 1, 3))


N_W1 = 6 * W_BR
N_W2 = 256
N_WT = 840
ROW_BIQ = W_BR
ROW_BV = W_BR + H_IDX * D_IDX
ROW_BIW = ROW_BV + DKV_B


def _in_proj_kernel(x_ref, mod_ref, w1_ref, w2_ref, wt_ref, *refs, first_tail_step, n_aliased):
    (aq_ref, ak_ref, av_ref, akb_ref, avb_ref, cq_ref, ckb_ref, cvb_ref, ckt_ref, cvt_ref,
     bk_ref, bv_ref, bik_ref, bkb_ref, bikb_ref, bqt_ref, biqt_ref, bvt_ref,
     biwt_ref) = refs[n_aliased:]
    x = x_ref[0]
    d = x.shape[-1]
    tm = x.shape[0]
    mod = mod_ref[0]
    shift = mod[:, :d]
    scale = mod[:, d:2 * d]
    h = (_layer_norm(x) * (1.0 + scale) + shift).astype(BF16)

    def proj(j):
        return _dot(h, w1_ref[:, j * W_BR:(j + 1) * W_BR])

    aq_ref[0] = proj(0).astype(BF16)
    ak = proj(1)
    ak_ref[...] = ak.reshape(ak_ref.shape)
    akb_ref[0] = ak.astype(BF16)
    av = proj(2)
    av_ref[...] = av.reshape(av_ref.shape)
    avb_ref[0] = av.astype(BF16)
    cq_ref[0] = proj(3).astype(BF16)
    ck = proj(4)
    ckb_ref[0] = ck.astype(BF16)
    cv = proj(5)
    cvb_ref[0] = cv.astype(BF16)

    @pl.when(pl.program_id(1) >= first_tail_step)
    def _():
        ckt_ref[0] = ck
        cvt_ref[0] = cv

    small = _dot(h, w2_ref[...])
    bk = small[:, 0:DKV_B]
    bik = small[:, 2 * DKV_B:3 * DKV_B]
    bk_ref[0] = bk
    bv_ref[0] = small[:, DKV_B:2 * DKV_B]
    bik_ref[0] = bik
    bkb_ref[0] = bk.astype(BF16)
    bikb_ref[0] = bik.astype(BF16)

    tr = _dot_nt(wt_ref[...], h)
    bqt_ref[0] = tr[0:ROW_BIQ].astype(BF16)
    biqt_ref[0] = tr[ROW_BIQ:ROW_BV].astype(BF16)
    bvt = tr[ROW_BV:ROW_BIW].astype(BF16)
    for c in range(tm // KEY_BLOCK):
        bvt_ref[0, c] = bvt[:, c * KEY_BLOCK:(c + 1) * KEY_BLOCK]
    biwt_ref[0] = tr[ROW_BIW:N_WT]


def _in_proj(x, mod, w1, w2, wt, tm, layer_stack=None):
    b, t, d = x.shape
    nt = t // tm
    tail = min(BAND_ROWS, t)
    tail_blocks = tail // tm
    first_tail = nt - tail_blocks
    row = lambda bb, i: (bb, i, 0)
    colmaj = lambda bb, i: (bb, 0, i)
    tail_map = lambda bb, i: (bb, jnp.maximum(i - first_tail, 0), 0)

    def rows(n, dt):
        return pl.BlockSpec((1, tm, n), row), jax.ShapeDtypeStruct((b, t, n), dt)

    outs = [
        rows(W_BR, BF16),
        rows(W_BR, F32), rows(W_BR, F32),
        rows(W_BR, BF16), rows(W_BR, BF16),
        rows(W_BR, BF16), rows(W_BR, BF16), rows(W_BR, BF16),
        (pl.BlockSpec((1, tm, W_BR), tail_map), jax.ShapeDtypeStruct((b, tail, W_BR), F32)),
        (pl.BlockSpec((1, tm, W_BR), tail_map), jax.ShapeDtypeStruct((b, tail, W_BR), F32)),
        rows(DKV_B, F32), rows(DKV_B, F32), rows(D_IDX, F32),
        rows(DKV_B, BF16), rows(D_IDX, BF16),
        (pl.BlockSpec((1, W_BR, tm), colmaj), jax.ShapeDtypeStruct((b, W_BR, t), BF16)),
        (pl.BlockSpec((1, H_IDX * D_IDX, tm), colmaj),
         jax.ShapeDtypeStruct((b, H_IDX * D_IDX, t), BF16)),
        (pl.BlockSpec((1, tm // KEY_BLOCK, DKV_B, KEY_BLOCK), lambda bb, i: (bb, i, 0, 0)),
         jax.ShapeDtypeStruct((b, t // KEY_BLOCK, DKV_B, KEY_BLOCK), BF16)),
        (pl.BlockSpec((1, N_WT - ROW_BIW, tm), colmaj),
         jax.ShapeDtypeStruct((b, N_WT - ROW_BIW, t), F32)),
    ]
    aliased, aliases = [], {}
    if layer_stack is not None:
        layer, depth, ak_stack, av_stack = layer_stack
        slab = (pl.BlockSpec((1, 1, tm, W_BR), lambda bb, i: (layer, bb, i, 0)),
                jax.ShapeDtypeStruct((depth, b, t, W_BR), F32))
        outs[1] = outs[2] = slab
        if ak_stack is not None:
            aliased = [ak_stack, av_stack]
            aliases = {5: 1, 6: 2}
    return pl.pallas_call(
        functools.partial(_in_proj_kernel, first_tail_step=first_tail, n_aliased=len(aliased)),
        grid=(b, nt),
        in_specs=[pl.BlockSpec((1, tm, d), row),
                  pl.BlockSpec((1, 1, mod.shape[-1]), lambda bb, i: (bb, 0, 0)),
                  pl.BlockSpec((d, N_W1), lambda bb, i: (0, 0)),
                  pl.BlockSpec((d, N_W2), lambda bb, i: (0, 0)),
                  pl.BlockSpec((N_WT, d), lambda bb, i: (0, 0))]
        + [pl.BlockSpec(memory_space=pl.ANY)] * len(aliased),
        out_specs=[o[0] for o in outs],
        out_shape=[o[1] for o in outs],
        input_output_aliases=aliases,
        compiler_params=_cparams(2),
        name="in_proj",
    )(x, mod, w1, w2, wt, *aliased)


SOFTPLUS_CUT = 30.0
STICK_TQ = 256
STICK_RUN = 2


def _stick_kernel(q_ref, kd_ref, vd_ref, km_ref, vm_ref, o_ref, carry_ref, acc_ref, *, n_main):
    tq = q_ref.shape[1]
    tk = KEY_BLOCK
    n_diag = tq // tk
    qb = pl.program_id(1)
    n_blocks = qb * n_diag if n_main is None else n_main

    def suffix_matrix(n):
        jj = lax.broadcasted_iota(jnp.int32, (n, n), 0)
        ss = lax.broadcasted_iota(jnp.int32, (n, n), 1)
        return jnp.where(jj >= ss, 1.0, 0.0).astype(BF16)

    upper_of = {nb: suffix_matrix(nb * tk) for nb in sorted({n_diag, STICK_RUN})}
    qi = lax.broadcasted_iota(jnp.int32, (tq, tk), 0)
    kj = lax.broadcasted_iota(jnp.int32, (tq, tk), 1)

    carry_ref[...] = jnp.zeros(carry_ref.shape, F32)
    acc_ref[...] = jnp.zeros(acc_ref.shape, F32)

    heads = range(N_HEADS)
    pairs = range(N_HEADS // 2)
    pair_lanes = [slice(p * 2 * D_HEAD, (p + 1) * 2 * D_HEAD) for p in pairs]
    first_of_pair = lax.broadcasted_iota(jnp.int32, (tk, 2 * D_HEAD), 1) < D_HEAD

    def head_pair_diag(x):
        zero = jnp.zeros_like(x)
        return jnp.concatenate([jnp.where(first_of_pair, x, zero),
                                jnp.where(first_of_pair, zero, x)], axis=0)

    def step(kblks, vblks, visibles):
        nb = len(kblks)
        zs = [[None] * N_HEADS for _ in range(nb)]
        for b in range(nb):
            for p in pairs:
                z2 = _dot_nt(q_ref[0, :, pair_lanes[p]], head_pair_diag(kblks[b][:, pair_lanes[p]]))
                zs[b][2 * p], zs[b][2 * p + 1] = z2[:, :tk], z2[:, tk:]
        sps = [[None] * N_HEADS for _ in range(nb)]
        for b in range(nb):
            for h in heads:
                z = zs[b][h]
                sp = jnp.maximum(z, jnp.log(1.0 + jnp.exp(jnp.minimum(z, SOFTPLUS_CUT))))
                if visibles[b] is not None:
                    sp = jnp.where(visibles[b], sp, 0.0)
                sps[b][h] = sp.astype(BF16)
        order = list(reversed(range(nb)))
        upper = upper_of[nb]
        srs = [_dot(jnp.concatenate([sps[b][h] for b in order], axis=1), upper) for h in heads]
        run_visible = None
        if visibles[0] is not None:
            run_visible = jnp.concatenate([visibles[b] for b in order], axis=1)
        weights = []
        for h in heads:
            later = carry_ref[h]
            z_run = jnp.concatenate([zs[b][h] for b in order], axis=1)
            a = jnp.exp(z_run - srs[h] - jnp.concatenate([later] * nb, axis=1))
            if run_visible is not None:
                a = jnp.where(run_visible, a, 0.0)
            weights.append(a.astype(BF16))
            carry_ref[h] = later + jnp.broadcast_to(srs[h][:, 0:1], (tq, tk))
        for p in pairs:
            w2 = jnp.concatenate([weights[2 * p], weights[2 * p + 1]], axis=1)
            vd = [head_pair_diag(vblks[b][:, pair_lanes[p]]) for b in order]
            v2 = jnp.concatenate([v[:tk] for v in vd] + [v[tk:] for v in vd], axis=0)
            acc_ref[p] += _dot(w2, v2)

    def diag_rows(dd):
        return slice(dd * tk, (dd + 1) * tk)

    diag = list(reversed(range(n_diag)))
    step([kd_ref[0, diag_rows(dd), :].astype(BF16) for dd in diag],
         [vd_ref[0, diag_rows(dd), :].astype(BF16) for dd in diag],
         [dd * tk + kj < qi for dd in diag])

    def key_rows(blk):
        return pl.ds(pl.multiple_of(blk * tk, tk), tk)

    def body(i, c):
        newest = n_blocks - 1 - STICK_RUN * i
        blks = [newest - j for j in range(STICK_RUN)]
        step([km_ref[0, key_rows(blk), :].astype(BF16) for blk in blks],
             [vm_ref[0, key_rows(blk), :].astype(BF16) for blk in blks], [None] * STICK_RUN)
        return c

    lax.fori_loop(0, n_blocks // STICK_RUN, body, 0)
    o_ref[0] = jnp.concatenate([acc_ref[p] for p in pairs], axis=1)


def _stick(q, k_diag, v_diag, k_main, v_main, *, tq, n_q_blocks, n_main):
    b = q.shape[0]
    tm = k_main.shape[1]
    blk = lambda bb, i: (bb, i, 0)
    whole = lambda bb, i: (bb, 0, 0)
    return pl.pallas_call(
        functools.partial(_stick_kernel, n_main=n_main),
        grid=(b, n_q_blocks),
        in_specs=[pl.BlockSpec((1, tq, W_BR), blk),
                  pl.BlockSpec((1, tq, W_BR), blk),
                  pl.BlockSpec((1, tq, W_BR), blk),
                  pl.BlockSpec((1, tm, W_BR), whole),
                  pl.BlockSpec((1, tm, W_BR), whole)],
        out_specs=pl.BlockSpec((1, tq, W_BR), blk),
        out_shape=jax.ShapeDtypeStruct((b, n_q_blocks * tq, W_BR), F32),
        scratch_shapes=[pltpu.VMEM((N_HEADS, tq, KEY_BLOCK), F32),
                        pltpu.VMEM((N_HEADS // 2, tq, 2 * D_HEAD), F32)],
        compiler_params=_cparams(2),
        name="stick_breaking",
    )(q, k_diag, v_diag, k_main, v_main)


def _flip_magnitude_if_negative(word):
    return word ^ (lax.shift_right_arithmetic(word, 31) & jnp.int32(0x7FFFFFFF))


def _key_to_float(key):
    return lax.bitcast_convert_type(_flip_magnitude_if_negative(key), F32)


def _float_to_key(x):
    return _flip_magnitude_if_negative(lax.bitcast_convert_type(x, jnp.int32))


DSA_TK = 2 * KEY_BLOCK
DSA_TQ = 256
ALIBI_ROWS = 64
BISECT_GROUP = 4
BISECT_FREE_GROUPS = 3


def _dsa_kernel(biqt_ref, biwt_ref, bqt_ref, ik_ref, k_ref, vt_ref, o_ref, sc_ref,
                *, n_full, q_off, n_valid, n_sel):
    tq = bqt_ref.shape[2]
    tk = DSA_TK
    sub = tk // KEY_BLOCK
    qb = pl.program_id(1)
    n_full_blocks = (qb * tq) // tk if n_full is None else n_full
    n_blocks = n_full_blocks + 1
    iw = biwt_ref[0]
    q0 = q_off + qb * tq

    kloc = lax.broadcasted_iota(jnp.int32, (tk, tq), 0)
    qpos = q0 + lax.broadcasted_iota(jnp.int32, (tk, tq), 1)

    def score_block(kb):
        start = pl.multiple_of(kb * tk, tk)
        ikb = ik_ref[0, pl.ds(start, tk), :].astype(BF16)
        sc = jnp.zeros((tk, tq), F32)
        for hp in range(H_IDX // 2):
            iq2 = jnp.concatenate(
                [biqt_ref[0, h * D_IDX:(h + 1) * D_IDX, :] for h in (2 * hp, 2 * hp + 1)], axis=1)
            lg2 = _dot(ikb, iq2)
            for u in range(2):
                h = 2 * hp + u
                sc = sc + iw[h:h + 1, :] * jnp.maximum(lg2[:, u * tq:(u + 1) * tq], 0.0)
        return jnp.where(sc == 0.0, 0.0, sc)

    def fill(j, c):
        kbs = [jnp.minimum(2 * j + u, n_full_blocks - 1) for u in range(2)]
        scores = [score_block(kb) for kb in kbs]
        for kb, sc in zip(kbs, scores):
            sc_ref[kb] = sc
        return c

    lax.fori_loop(0, (n_full_blocks + 1) // 2, fill, 0)
    kpos_last = n_full_blocks * tk + kloc
    admissible = ((lax.shift_right_logical(kpos_last, 6) <= lax.shift_right_logical(qpos, 6))
                  & (kpos_last < n_valid))
    sc_ref[n_full_blocks] = jnp.where(admissible, score_block(n_full_blocks), -jnp.inf)
    sc_ref[n_blocks] = jnp.full((tk, tq), -jnp.inf, F32)
    n_pairs = (n_blocks + 1) // 2

    def count(pred):
        def body(j, acc):
            for u in range(2):
                for r in range(sub):
                    x = sc_ref[2 * j + u, r * KEY_BLOCK:(r + 1) * KEY_BLOCK, :]
                    acc = acc + jnp.where(pred(x), 1.0, 0.0)
            return acc
        acc = lax.fori_loop(0, n_pairs, body, jnp.zeros((KEY_BLOCK, tq), F32))
        return jnp.sum(acc, axis=0, keepdims=True)

    k_sel = jnp.float32(n_sel)

    n_ge0 = count(lambda x: x >= 0.0)
    n_gt0 = count(lambda x: x > 0.0)
    zero_thr = (n_ge0 >= k_sel) & (n_gt0 < k_sel)
    positive = n_gt0 >= k_sel
    assert tk >= n_sel
    gmax = lax.fori_loop(0, n_blocks, lambda kb, g: jnp.maximum(g, sc_ref[kb]),
                         jnp.full((tk, tq), -jnp.inf, F32))
    lo_b = jnp.maximum(_float_to_key(jnp.min(gmax, axis=0, keepdims=True)), KEY_LO)
    hi_b = jnp.minimum(_float_to_key(jnp.max(gmax, axis=0, keepdims=True)), KEY_HI - 1) + 1
    lo0 = jnp.where(zero_thr, 0, jnp.where(positive, jnp.maximum(lo_b, 1), lo_b))
    hi0 = jnp.where(zero_thr, 1, jnp.where(positive, hi_b, jnp.minimum(hi_b, 0)))

    def unresolved(lo, hi):
        return (hi - lo) != 1

    def bisect_cond(st):
        it, lo, hi = st
        return ((it < 32 // BISECT_GROUP)
                & (jnp.max(jnp.where(unresolved(lo, hi), 1.0, 0.0)) > 0.5))

    def bisect(st):
        it, lo, hi = st
        for _ in range(BISECT_GROUP):
            active = unresolved(lo, hi)
            mid = lo + lax.shift_right_logical(hi - lo, 1)
            n_ge = count(lambda x, mid=mid: x >= _key_to_float(mid))
            ge = n_ge >= k_sel
            exact = n_ge == k_sel
            lo, hi = (jnp.where(active & ge, mid, lo),
                      jnp.where(active, jnp.where(exact, mid + 1, jnp.where(ge, hi, mid)), hi))
        return it + 1, lo, hi

    st = lax.fori_loop(0, BISECT_FREE_GROUPS, lambda _, s: bisect(s), (jnp.int32(0), lo0, hi0))
    _, lo, _ = lax.while_loop(bisect_cond, bisect, st)
    thr = _key_to_float(lo)
    need = k_sel - count(lambda x: x > thr)

    srow = lax.broadcasted_iota(jnp.int32, (tk, tk), 0)
    jcol = lax.broadcasted_iota(jnp.int32, (tk, tk), 1)
    before = jnp.where(jcol < srow, 1.0, 0.0).astype(BF16)

    def select(j, seen):
        xs = [sc_ref[2 * j + u] for u in range(2)]
        eqs = [x == thr for x in xs]
        eqfs = [jnp.where(eq, 1.0, 0.0) for eq in eqs]
        within = [_dot(before, eqf.astype(BF16)) for eqf in eqfs]
        for u in range(2):
            chosen = (xs[u] > thr) | (eqs[u] & (within[u] + seen < need))
            sc_ref[2 * j + u] = jnp.where(chosen, 0.0, NEG_INF)
            seen = seen + jnp.sum(eqfs[u], axis=0, keepdims=True)
        return seen

    lax.fori_loop(0, n_pairs, select, jnp.zeros((1, tq), F32))

    q_all = jnp.concatenate(
        [bqt_ref[0, h * D_HEAD:(h + 1) * D_HEAD, :] for h in range(N_HEADS)], axis=1)
    slope = jnp.concatenate(
        [jnp.full((1, tq), 2.0 ** -(h + 1), F32) for h in range(N_HEADS)], axis=1)
    qi_all = jnp.concatenate(
        [lax.broadcasted_iota(jnp.int32, (1, tq), 1)] * N_HEADS, axis=1).astype(F32)
    arow = lax.broadcasted_iota(jnp.int32, (ALIBI_ROWS, N_HEADS * tq), 0)
    q_extra = jnp.where(arow == 0, -slope * qi_all, jnp.where(arow <= 2, slope, 0.0))
    q_aug = jnp.concatenate([q_all, q_extra.astype(BF16)], axis=0)
    acol = lax.broadcasted_iota(jnp.int32, (tk, ALIBI_ROWS), 1)
    kj_f = lax.broadcasted_iota(jnp.int32, (tk, ALIBI_ROWS), 0).astype(F32)

    def load_kv(kb):
        start = pl.multiple_of(kb * tk, tk)
        kblk = k_ref[0, pl.ds(start, tk), :].astype(BF16)
        vblk_t = jnp.concatenate(
            [vt_ref[0, kb * sub + j].astype(BF16) for j in range(sub)], axis=1)
        return kblk, vblk_t

    def softmax_step(st, s_heads, vblk_t):
        ms, ls, accs = st
        m_new, new_l, new_acc = [], [], []
        for hp in range(N_HEADS // 2):
            hs = (2 * hp, 2 * hp + 1)
            m2 = [jnp.maximum(ms[h], jnp.max(s_heads[h], axis=0, keepdims=True)) for h in hs]
            p2 = [jnp.exp(s_heads[h] - m2[u]) for u, h in enumerate(hs)]
            pv2 = _dot(vblk_t, jnp.concatenate([p.astype(BF16) for p in p2], axis=1))
            for u, h in enumerate(hs):
                alpha = jnp.exp(ms[h] - m2[u])
                m_new.append(m2[u])
                new_l.append(alpha * ls[h] + jnp.sum(p2[u], axis=0, keepdims=True))
                new_acc.append(alpha * accs[h] + pv2[:, u * tq:(u + 1) * tq])
        return tuple(m_new), tuple(new_l), tuple(new_acc)

    def per_head_logits(keys, queries):
        tiles = []
        for hp in range(N_HEADS // 2):
            s2 = _dot(keys, queries[:, 2 * hp * tq:(2 * hp + 2) * tq])
            tiles += [s2[:, :tq], s2[:, tq:]]
        return tiles

    def full_block_logits(kb, mb):
        kblk, vblk_t = load_kv(kb)
        neg_delta = (kb * tk - q0).astype(F32)
        k_extra = jnp.where(acol == 0, 1.0,
                            jnp.where(acol == 1, kj_f, jnp.where(acol == 2, neg_delta, 0.0)))
        k_aug = jnp.concatenate([kblk, k_extra.astype(BF16)], axis=1)
        return [s + mb for s in per_head_logits(k_aug, q_aug)], vblk_t

    def attend_pair(parts, st):
        s_heads = [jnp.concatenate([parts[0][0][h], parts[1][0][h]], axis=0)
                   for h in range(N_HEADS)]
        return softmax_step(st, s_heads, jnp.concatenate([parts[0][1], parts[1][1]], axis=1))

    def attend_full_pair(j, st):
        return attend_pair([full_block_logits(2 * j + u, sc_ref[2 * j + u]) for u in range(2)], st)

    st = lax.fori_loop(
        0, n_full_blocks // 2, attend_full_pair,
        (tuple(jnp.full((1, tq), -3e38, F32) for _ in range(N_HEADS)),
         tuple(jnp.zeros((1, tq), F32) for _ in range(N_HEADS)),
         tuple(jnp.zeros((DKV_B, tq), F32) for _ in range(N_HEADS))))

    odd = (n_full_blocks % 2) == 1
    spare = jnp.maximum(n_full_blocks - 1, 0)
    spare_part = full_block_logits(spare, jnp.where(odd, sc_ref[spare], NEG_INF))
    kblk, vblk_t = load_kv(n_full_blocks)
    dist = jnp.abs(qpos - kpos_last).astype(F32)
    mb = sc_ref[n_full_blocks]
    last_part = ([s + (mb - (2.0 ** -(h + 1)) * dist)
                  for h, s in enumerate(per_head_logits(kblk, q_all))], vblk_t)
    _, ls, accs = attend_pair([spare_part, last_part], st)

    o_ref[0] = jnp.concatenate([(accs[h] / ls[h]).T for h in range(N_HEADS)], axis=1)


def _dsa(biqt, biwt, bqt, ik, k, vt, *, tq, n_q_blocks, n_full, q_off, n_valid, n_sel):
    b = bqt.shape[0]
    tk_total = ik.shape[1]
    colmaj = lambda bb, i: (bb, 0, i)
    whole3 = lambda bb, i: (bb, 0, 0)
    return pl.pallas_call(
        functools.partial(_dsa_kernel, n_full=n_full, q_off=q_off, n_valid=n_valid, n_sel=n_sel),
        grid=(b, n_q_blocks),
        in_specs=[pl.BlockSpec((1, H_IDX * D_IDX, tq), colmaj),
                  pl.BlockSpec((1, biwt.shape[1], tq), colmaj),
                  pl.BlockSpec((1, W_BR, tq), colmaj),
                  pl.BlockSpec((1, tk_total, D_IDX), whole3),
                  pl.BlockSpec((1, tk_total, DKV_B), whole3),
                  pl.BlockSpec((1, tk_total // KEY_BLOCK, DKV_B, KEY_BLOCK),
                               lambda bb, i: (bb, 0, 0, 0))],
        out_specs=pl.BlockSpec((1, tq, W_BR), lambda bb, i: (bb, i, 0)),
        out_shape=jax.ShapeDtypeStruct((b, n_q_blocks * tq, W_BR), F32),
        scratch_shapes=[pltpu.VMEM((tk_total // DSA_TK + 1, DSA_TK, tq), F32)],
        compiler_params=_cparams(2),
        name="dsa",
    )(biqt, biwt, bqt, ik, k, vt)


def _band_kernel(*refs, n_win, win_start_blocks, n_valid_win):
    q_ref = refs[0]
    k_refs = refs[1:1 + n_win]
    v_refs = refs[1 + n_win:1 + 2 * n_win]
    bias_ref = refs[1 + 2 * n_win]
    o_ref = refs[2 + 2 * n_win]
    tq = q_ref.shape[1]
    w = n_win * tq
    i = pl.program_id(1)
    col = lax.broadcasted_iota(jnp.int32, (1, w), 1)
    if win_start_blocks is None:
        valid = col < n_valid_win
    else:
        valid = ((i + win_start_blocks) * tq + col >= 0) & (col < n_valid_win)
    kwin = jnp.concatenate([r[0] for r in k_refs], axis=0)
    vwin = jnp.concatenate([r[0] for r in v_refs], axis=0)
    outs = []
    for h in range(N_HEADS):
        lanes = slice(h * D_HEAD, (h + 1) * D_HEAD)
        s = _dot_nt(q_ref[0, :, lanes], kwin[:, lanes]) + bias_ref[h]
        s = jnp.where(valid, s, NEG_INF)
        e = jnp.exp(s - jnp.max(s, axis=-1, keepdims=True))
        den = jnp.sum(e, axis=-1, keepdims=True)
        outs.append(_dot(e.astype(BF16), vwin[:, lanes]) / den)
    o_ref[0] = jnp.concatenate(outs, axis=1)


def _band(q, k, v, bias, *, tq, n_q_blocks, sliding, n_valid_win):
    b = q.shape[0]
    n_win = bias.shape[2] // tq
    assert bias.shape[1] == tq

    def kv_map(wi):
        if sliding:
            return lambda bb, i: (bb, jnp.maximum(i - (n_win - 1) + wi, 0), 0)
        return lambda bb, i: (bb, wi, 0)

    kv_specs = [pl.BlockSpec((1, tq, W_BR), kv_map(wi)) for wi in range(n_win)]
    return pl.pallas_call(
        functools.partial(_band_kernel, n_win=n_win,
                          win_start_blocks=-(n_win - 1) if sliding else None,
                          n_valid_win=n_valid_win),
        grid=(b, n_q_blocks),
        in_specs=([pl.BlockSpec((1, tq, W_BR), lambda bb, i: (bb, i, 0))] + kv_specs + kv_specs
                  + [pl.BlockSpec(bias.shape, lambda bb, i: (0, 0, 0))]),
        out_specs=pl.BlockSpec((1, tq, W_BR), lambda bb, i: (bb, i, 0)),
        out_shape=jax.ShapeDtypeStruct((b, n_q_blocks * tq, W_BR), F32),
        compiler_params=_cparams(2),
        name="band",
    )(q, *([k] * n_win), *([v] * n_win), bias)


def _merge_kernel(x_ref, mod_ref, ya_ref, yb_ref, yc_ref, wz_ref, wg_ref, bg_ref, wbr_ref,
                  wo_ref, lng_ref, lnb_ref, o_ref, *, alpha):
    x = x_ref[0]
    d = x.shape[-1]
    mod = mod_ref[0]
    shift = mod[:, :d]
    scale = mod[:, d:2 * d]
    gate = mod[:, 2 * d:3 * d]
    h = (_layer_norm(x) * (1.0 + scale) + shift).astype(BF16)
    m = jnp.zeros(x.shape, F32)
    for n, y_ref in enumerate((ya_ref, yb_ref, yc_ref)):
        z = _dot(h, wz_ref[:, n * W_BR:(n + 1) * W_BR])
        u = (y_ref[0] * (z * _sigmoid(z))).astype(BF16)
        t = _dot(u, wbr_ref[n])
        g = _sigmoid(_dot(h, wg_ref[n]) + bg_ref[n])
        m = m + g * t
    out = _dot(m.astype(BF16), wo_ref[...])
    r = alpha * x + gate * out
    o_ref[0] = _layer_norm(r) * lng_ref[...] + lnb_ref[...]


def _merge(x, mod, ya, yb, yc, wz, wg, bg, wbr, wo, lng, lnb, *, tm, alpha):
    b, t, d = x.shape
    row = lambda bb, i: (bb, i, 0)
    const2 = lambda bb, i: (0, 0)
    const3 = lambda bb, i: (0, 0, 0)
    return pl.pallas_call(
        functools.partial(_merge_kernel, alpha=alpha),
        grid=(b, t // tm),
        in_specs=[pl.BlockSpec((1, tm, d), row),
                  pl.BlockSpec((1, 1, mod.shape[-1]), lambda bb, i: (bb, 0, 0)),
                  pl.BlockSpec((1, tm, W_BR), row),
                  pl.BlockSpec((1, tm, W_BR), row),
                  pl.BlockSpec((1, tm, W_BR), row),
                  pl.BlockSpec(wz.shape, const2),
                  pl.BlockSpec(wg.shape, const3),
                  pl.BlockSpec(bg.shape, const3),
                  pl.BlockSpec(wbr.shape, const3),
                  pl.BlockSpec(wo.shape, const2),
                  pl.BlockSpec(lng.shape, const2),
                  pl.BlockSpec(lnb.shape, const2)],
        out_specs=pl.BlockSpec((1, tm, d), row),
        out_shape=jax.ShapeDtypeStruct((b, t, d), F32),
        compiler_params=_cparams(2),
        name="merge",
    )(x, mod, ya, yb, yc, wz, wg, bg, wbr, wo, lng, lnb)


def _split_w_in(w):
    sizes = (W_BR, W_BR, W_BR, W_BR, W_BR, DKV_B, DKV_B, W_BR, H_IDX * D_IDX, D_IDX, H_IDX,
             W_BR, W_BR, W_BR, W_BR)
    cols, off = [], 0
    for s in sizes:
        cols.append(w[:, off:off + s])
        off += s
    aq, ak, av, az, bq, bk, bv, bz, biq, bik, biw, cq, ck, cv, cz = cols
    d = w.shape[0]
    w1 = jnp.concatenate([aq * ATT_SCALE, ak, av, cq * ATT_SCALE, ck, cv], axis=1).astype(BF16)
    w2 = jnp.concatenate([bk, bv, bik, jnp.zeros((d, N_W2 - 3 * DKV_B), w.dtype)],
                         axis=1).astype(BF16)
    wt = jnp.concatenate([bq * ATT_SCALE, biq * (D_IDX ** -0.5), bv, biw * IDX_HEAD_SCALE,
                          jnp.zeros((d, N_WT - ROW_BIW - H_IDX), w.dtype)], axis=1).T.astype(BF16)
    wz = jnp.concatenate([az, bz, cz], axis=1).astype(BF16)
    return w1, w2, wt, wz


def _key_blocks_t(v):
    b, t, n = v.shape
    return jnp.transpose(v.reshape(b, t // KEY_BLOCK, KEY_BLOCK, n), (0, 1, 3, 2))


def kernel(x_prompt, x_sample, c_prompt, c_sample, cache_a_k, cache_a_v, cache_b_k, cache_b_v,
           cache_b_kidx, cache_c_k, cache_c_v, w_ada, b_ada, w_in, w_gate, b_gate, w_branch,
           w_out, rel_bias, ln_g, ln_b):
    depth = w_in.shape[0]
    bp, seq, d = x_prompt.shape
    bs, dec_seq, _ = x_sample.shape
    past_len = cache_a_k.shape[2]
    band_rows = cache_c_k.shape[2]
    alpha = (2.0 * depth) ** 0.25
    assert seq % 512 == 0 and dec_seq <= CHUNK and past_len % DSA_TK == 0
    assert band_rows == BAND_ROWS and past_len // CHUNK == (past_len + dec_seq - 1) // CHUNK

    mod_all = _modulation(jnp.concatenate([c_prompt, c_sample], axis=0), w_ada, b_ada)
    bias_all = _band_bias(rel_bias)

    xp = x_prompt
    xs = jnp.pad(x_sample, ((0, 0), (0, SAMPLE_PAD - dec_seq), (0, 0)))
    n_sel_p = min(TOPK_MAX, seq // 4)
    n_sel_s = min(TOPK_MAX, (past_len + dec_seq) // 4)
    tm_p = 512
    st_p = [[] for _ in range(5)]
    st_s = [[] for _ in range(7)]
    ak_stack = av_stack = None

    for l in range(depth):
        w1, w2, wt, wz = _split_w_in(w_in[l])
        wg = w_gate[l].astype(BF16)
        bg = b_gate[l].reshape(N_BRANCH, 1, d)
        wbr = w_branch[l].astype(BF16)
        wo = w_out[l].astype(BF16)
        lng = ln_g[l].reshape(1, d)
        lnb = ln_b[l].reshape(1, d)
        mod_p = mod_all[l, :bp].reshape(bp, 1, 3 * d)
        mod_s = mod_all[l, bp:].reshape(bs, 1, 3 * d)
        bias_p = bias_all[l]

        (aq, ak_stack, av_stack, akb, avb, cq, ckb, cvb, ckt, cvt, bk, bv, bik, bkb, bikb,
         bqt, biqt, bvt, biwt) = _in_proj(xp, mod_p, w1, w2, wt, tm_p,
                                          layer_stack=(l, depth, ak_stack, av_stack))
        ya = _stick(aq, akb, avb, akb, avb, tq=STICK_TQ, n_q_blocks=seq // STICK_TQ, n_main=None)
        yb = _dsa(biqt, biwt, bqt, bikb, bkb, bvt, tq=DSA_TQ, n_q_blocks=seq // DSA_TQ,
                  n_full=None, q_off=0, n_valid=seq, n_sel=n_sel_p)
        yc = _band(cq, ckb, cvb, bias_p, tq=BAND_TQ, n_q_blocks=seq // BAND_TQ, sliding=True,
                   n_valid_win=BAND_W)
        xp = _merge(xp, mod_p, ya, yb, yc, wz, wg, bg, wbr, wo, lng, lnb, tm=512, alpha=alpha)
        for lst, val in zip(st_p, (bk, bv, bik, ckt.reshape(bp, -1, N_HEADS, D_HEAD),
                                   cvt.reshape(bp, -1, N_HEADS, D_HEAD))):
            lst.append(val)

        (aq, ak, av, akb, avb, cq, ckb, cvb, ckt, cvt, bk, bv, bik, bkb, bikb,
         bqt, biqt, bvt, biwt) = _in_proj(xs, mod_s, w1, w2, wt, SAMPLE_PAD)
        ya = _stick(aq, akb, avb, cache_a_k[l].reshape(bs, past_len, W_BR),
                    cache_a_v[l].reshape(bs, past_len, W_BR),
                    tq=SAMPLE_PAD, n_q_blocks=1, n_main=past_len // KEY_BLOCK)
        key_pad = jnp.zeros((bs, DSA_TK - SAMPLE_PAD, DKV_B), BF16)
        ik_cat = jnp.concatenate([cache_b_kidx[l].astype(BF16), bikb, key_pad], axis=1)
        k_cat = jnp.concatenate([cache_b_k[l].astype(BF16), bkb, key_pad], axis=1)
        vt_cat = jnp.concatenate([_key_blocks_t(cache_b_v[l].astype(BF16)), bvt,
                                  _key_blocks_t(key_pad)], axis=1)
        yb = _dsa(biqt, biwt, bqt, ik_cat, k_cat, vt_cat, tq=SAMPLE_PAD, n_q_blocks=1,
                  n_full=past_len // DSA_TK, q_off=past_len, n_valid=past_len + dec_seq,
                  n_sel=n_sel_s)
        kc_cat = jnp.concatenate(
            [cache_c_k[l].reshape(bs, band_rows, W_BR).astype(BF16), ckb], axis=1)
        vc_cat = jnp.concatenate(
            [cache_c_v[l].reshape(bs, band_rows, W_BR).astype(BF16), cvb], axis=1)
        yc = _band(cq, kc_cat, vc_cat, bias_p[:, :SAMPLE_PAD, :band_rows + SAMPLE_PAD],
                   tq=SAMPLE_PAD, n_q_blocks=1, sliding=False, n_valid_win=band_rows + dec_seq)
        xs = _merge(xs, mod_s, ya, yb, yc, wz, wg, bg, wbr, wo, lng, lnb, tm=SAMPLE_PAD,
                    alpha=alpha)
        new_ck = jnp.concatenate(
            [cache_c_k[l], ckt[:, :dec_seq].reshape(bs, dec_seq, N_HEADS, D_HEAD)],
            axis=1)[:, -band_rows:]
        new_cv = jnp.concatenate(
            [cache_c_v[l], cvt[:, :dec_seq].reshape(bs, dec_seq, N_HEADS, D_HEAD)],
            axis=1)[:, -band_rows:]
        for lst, val in zip(st_s, (ak[:, :dec_seq].reshape(bs, dec_seq, N_HEADS, D_HEAD),
                                   av[:, :dec_seq].reshape(bs, dec_seq, N_HEADS, D_HEAD),
                                   bk[:, :dec_seq], bv[:, :dec_seq], bik[:, :dec_seq],
                                   new_ck, new_cv)):
            lst.append(val)

    return (xp, xs[:, :dec_seq],
            ak_stack.reshape(depth, bp, seq, N_HEADS, D_HEAD),
            av_stack.reshape(depth, bp, seq, N_HEADS, D_HEAD),
            *[jnp.stack(v) for v in st_p], *[jnp.stack(v) for v in st_s])
```

```python
import functools

import jax
import jax.numpy as jnp
from jax import lax
from jax.experimental import pallas as pl
from jax.experimental.pallas import tpu as pltpu

F32 = jnp.float32
BF16 = jnp.bfloat16

CHUNK = 64
D_HEAD = 64
N_HEADS = 8
W_BR = N_HEADS * D_HEAD
DKV_B = 64
H_IDX = 4
D_IDX = 64
TOPK_MAX = 256
BAND_CHUNKS = 8
BAND_ROWS = BAND_CHUNKS * CHUNK
REL_CLIP = 128
N_BRANCH = 3
LN_EPS = 1e-5
NEG_INF = -1e30
ATT_SCALE = D_HEAD ** -0.5
IDX_HEAD_SCALE = H_IDX ** -0.5

LANE = 128
KEY_BLOCK = 128
SAMPLE_PAD = 128
VMEM_LIMIT = 56 * 1024 * 1024

KEY_LO = -2139095040
KEY_HI = 2139095040


def _cparams(n_axes):
    return pltpu.CompilerParams(
        dimension_semantics=("arbitrary",) * n_axes, vmem_limit_bytes=VMEM_LIMIT)


def _sigmoid(v):
    return 1.0 / (1.0 + jnp.exp(-v))


def _layer_norm(x):
    mu = jnp.mean(x, axis=-1, keepdims=True)
    xc = x - mu
    var = jnp.mean(xc * xc, axis=-1, keepdims=True)
    return xc * lax.rsqrt(var + LN_EPS)


def _dot(a, b):
    return jnp.dot(a, b, preferred_element_type=F32)


def _dot_nt(a, b):
    return lax.dot_general(a, b, (((1,), (1,)), ((), ())), preferred_element_type=F32)


def _mod_kernel(c_ref, w_ref, b_ref, o_ref):
    c = c_ref[...]
    s = c * _sigmoid(c)
    o_ref[0] = jnp.dot(s, w_ref[0], preferred_element_type=F32,
                       precision=lax.Precision.HIGHEST) + b_ref[0]


def _modulation(c_all, w_ada, b_ada):
    depth, d, d3 = w_ada.shape
    n = c_all.shape[0]
    tn = 1024
    return pl.pallas_call(
        _mod_kernel,
        grid=(depth, d3 // tn),
        in_specs=[pl.BlockSpec((n, d), lambda l, j: (0, 0)),
                  pl.BlockSpec((1, d, tn), lambda l, j: (l, 0, j)),
                  pl.BlockSpec((1, 1, tn), lambda l, j: (l, 0, j))],
        out_specs=pl.BlockSpec((1, n, tn), lambda l, j: (l, 0, j)),
        out_shape=jax.ShapeDtypeStruct((depth, n, d3), F32),
        compiler_params=_cparams(2),
        name="modulation",
    )(c_all, w_ada, b_ada.reshape(depth, 1, d3))


BAND_TQ = 256
BAND_W = BAND_ROWS + BAND_TQ
BIAS_ROWS_PER_STEP = 8
REL_PAD = 384


BAND_KEYS = BAND_ROWS + CHUNK
BAND_KEYS_PAD = 640
BAND_Q_CHUNKS = BAND_TQ // CHUNK


def _band_bias_kernel(tab_ref, o_ref):
    q0 = pl.program_id(1) * BIAS_ROWS_PER_STEP
    tab = tab_ref[0]
    col = lax.broadcasted_iota(jnp.int32, (1, BAND_KEYS_PAD), 1)
    sub = lax.broadcasted_iota(jnp.int32, (REL_PAD, BAND_KEYS_PAD), 0)
    nh = tab.shape[0]
    for rr in range(BIAS_ROWS_PER_STEP):
        idx = jnp.clip(q0 + rr - col + BAND_ROWS, -REL_CLIP, REL_CLIP) + REL_CLIP
        onehot = jnp.where(sub == idx, 1.0, 0.0).astype(F32)
        vals = jnp.dot(tab, onehot, preferred_element_type=F32,
                       precision=lax.Precision.HIGHEST)[:, :BAND_KEYS]
        for cc in range(BAND_Q_CHUNKS):
            left = jnp.full((nh, cc * CHUNK), NEG_INF, F32)
            right = jnp.full((nh, BAND_W - BAND_KEYS - cc * CHUNK), NEG_INF, F32)
            parts = [p for p in (left, vals, right) if p.shape[1] > 0]
            o_ref[0, cc, rr] = jnp.concatenate(parts, axis=1)


def _band_bias(rel_bias):
    depth, nh, nrel = rel_bias.shape
    tab = jnp.pad(rel_bias, ((0, 0), (0, 0), (0, REL_PAD - nrel)))
    out = pl.pallas_call(
        _band_bias_kernel,
        grid=(depth, CHUNK // BIAS_ROWS_PER_STEP),
        in_specs=[pl.BlockSpec((1, nh, REL_PAD), lambda l, i: (l, 0, 0))],
        out_specs=pl.BlockSpec((1, BAND_Q_CHUNKS, BIAS_ROWS_PER_STEP, nh, BAND_W),
                               lambda l, i: (l, 0, i, 0, 0)),
        out_shape=jax.ShapeDtypeStruct((depth, BAND_Q_CHUNKS, CHUNK, nh, BAND_W), F32),
        compiler_params=_cparams(2),
        name="band_bias",
    )(tab)
    out = out.reshape(depth, BAND_TQ, nh, BAND_W)
    return jnp.transpose(out, (0, 2, 1, 3))


N_W1 = 6 * W_BR
N_W2 = 256
N_WT = 840
ROW_BIQ = W_BR
ROW_BV = W_BR + H_IDX * D_IDX
ROW_BIW = ROW_BV + DKV_B


def _in_proj_kernel(x_ref, mod_ref, w1_ref, w2_ref, wt_ref, *refs, first_tail_step, n_prev):
    prev_ak, prev_av = refs[:n_prev], refs[n_prev:2 * n_prev]
    (aq_ref, ak_ref, av_ref, akb_ref, avb_ref, cq_ref, ckb_ref, cvb_ref, ckt_ref, cvt_ref,
     bk_ref, bv_ref, bik_ref, bkb_ref, bikb_ref, bqt_ref, biqt_ref, bvt_ref,
     biwt_ref) = refs[2 * n_prev:]
    x = x_ref[0]
    d = x.shape[-1]
    tm = x.shape[0]
    mod = mod_ref[0]
    shift = mod[:, :d]
    scale = mod[:, d:2 * d]
    h = (_layer_norm(x) * (1.0 + scale) + shift).astype(BF16)

    def proj(j):
        return _dot(h, w1_ref[:, j * W_BR:(j + 1) * W_BR])

    aq_ref[0] = proj(0).astype(BF16)
    ak = proj(1)
    akb_ref[0] = ak.astype(BF16)
    av = proj(2)
    if n_prev:
        for j in range(n_prev):
            ak_ref[j, 0] = prev_ak[j][0]
            av_ref[j, 0] = prev_av[j][0]
        ak_ref[n_prev, 0] = ak
        av_ref[n_prev, 0] = av
    else:
        ak_ref[0] = ak
        av_ref[0] = av
    avb_ref[0] = av.astype(BF16)
    cq_ref[0] = proj(3).astype(BF16)
    ck = proj(4)
    ckb_ref[0] = ck.astype(BF16)
    cv = proj(5)
    cvb_ref[0] = cv.astype(BF16)

    @pl.when(pl.program_id(1) >= first_tail_step)
    def _():
        ckt_ref[0] = ck
        cvt_ref[0] = cv

    small = _dot(h, w2_ref[...])
    bk = small[:, 0:DKV_B]
    bik = small[:, 2 * DKV_B:3 * DKV_B]
    bk_ref[0] = bk
    bv_ref[0] = small[:, DKV_B:2 * DKV_B]
    bik_ref[0] = bik
    bkb_ref[0] = bk.astype(BF16)
    bikb_ref[0] = bik.astype(BF16)

    tr = _dot_nt(wt_ref[...], h)
    bqt_ref[0] = tr[0:ROW_BIQ].astype(BF16)
    biqt_ref[0] = tr[ROW_BIQ:ROW_BV].astype(BF16)
    bvt = tr[ROW_BV:ROW_BIW].astype(BF16)
    for c in range(tm // KEY_BLOCK):
        bvt_ref[0, c] = bvt[:, c * KEY_BLOCK:(c + 1) * KEY_BLOCK]
    biwt_ref[0] = tr[ROW_BIW:N_WT]


def _in_proj(x, mod, w1, w2, wt, tm, gather_layers=None):
    b, t, d = x.shape
    nt = t // tm
    tail = min(BAND_ROWS, t)
    tail_blocks = tail // tm
    first_tail = nt - tail_blocks
    row = lambda bb, i: (bb, i, 0)
    colmaj = lambda bb, i: (bb, 0, i)
    tail_map = lambda bb, i: (bb, jnp.maximum(i - first_tail, 0), 0)

    def rows(n, dt):
        return pl.BlockSpec((1, tm, n), row), jax.ShapeDtypeStruct((b, t, n), dt)

    outs = [
        rows(W_BR, BF16),
        rows(W_BR, F32), rows(W_BR, F32),
        rows(W_BR, BF16), rows(W_BR, BF16),
        rows(W_BR, BF16), rows(W_BR, BF16), rows(W_BR, BF16),
        (pl.BlockSpec((1, tm, W_BR), tail_map), jax.ShapeDtypeStruct((b, tail, W_BR), F32)),
        (pl.BlockSpec((1, tm, W_BR), tail_map), jax.ShapeDtypeStruct((b, tail, W_BR), F32)),
        rows(DKV_B, F32), rows(DKV_B, F32), rows(D_IDX, F32),
        rows(DKV_B, BF16), rows(D_IDX, BF16),
        (pl.BlockSpec((1, W_BR, tm), colmaj), jax.ShapeDtypeStruct((b, W_BR, t), BF16)),
        (pl.BlockSpec((1, H_IDX * D_IDX, tm), colmaj),
         jax.ShapeDtypeStruct((b, H_IDX * D_IDX, t), BF16)),
        (pl.BlockSpec((1, tm // KEY_BLOCK, DKV_B, KEY_BLOCK), lambda bb, i: (bb, i, 0, 0)),
         jax.ShapeDtypeStruct((b, t // KEY_BLOCK, DKV_B, KEY_BLOCK), BF16)),
        (pl.BlockSpec((1, N_WT - ROW_BIW, tm), colmaj),
         jax.ShapeDtypeStruct((b, N_WT - ROW_BIW, t), F32)),
    ]
    prev_ak, prev_av = gather_layers if gather_layers is not None else ((), ())
    n_prev = len(prev_ak)
    if n_prev:
        stacked = (pl.BlockSpec((n_prev + 1, 1, tm, W_BR), lambda bb, i: (0, bb, i, 0)),
                   jax.ShapeDtypeStruct((n_prev + 1, b, t, W_BR), F32))
        outs[1] = outs[2] = stacked
    return pl.pallas_call(
        functools.partial(_in_proj_kernel, first_tail_step=first_tail, n_prev=n_prev),
        grid=(b, nt),
        in_specs=[pl.BlockSpec((1, tm, d), row),
                  pl.BlockSpec((1, 1, mod.shape[-1]), lambda bb, i: (bb, 0, 0)),
                  pl.BlockSpec((d, N_W1), lambda bb, i: (0, 0)),
                  pl.BlockSpec((d, N_W2), lambda bb, i: (0, 0)),
                  pl.BlockSpec((N_WT, d), lambda bb, i: (0, 0))]
        + [pl.BlockSpec((1, tm, W_BR), row)] * (2 * n_prev),
        out_specs=[o[0] for o in outs],
        out_shape=[o[1] for o in outs],
        compiler_params=_cparams(2),
        name="in_proj",
    )(x, mod, w1, w2, wt, *prev_ak, *prev_av)


SOFTPLUS_CUT = 30.0
STICK_TQ = 256
STICK_RUN = 2


def _stick_kernel(q_ref, kd_ref, vd_ref, km_ref, vm_ref, o_ref, carry_ref, acc_ref, *, n_main):
    tq = q_ref.shape[1]
    tk = KEY_BLOCK
    n_diag = tq // tk
    qb = pl.program_id(1)
    n_blocks = qb * n_diag if n_main is None else n_main

    def suffix_matrix(n):
        jj = lax.broadcasted_iota(jnp.int32, (n, n), 0)
        ss = lax.broadcasted_iota(jnp.int32, (n, n), 1)
        return jnp.where(jj >= ss, 1.0, 0.0).astype(BF16)

    upper_of = {nb: suffix_matrix(nb * tk) for nb in sorted({n_diag, STICK_RUN})}
    qi = lax.broadcasted_iota(jnp.int32, (tq, tk), 0)
    kj = lax.broadcasted_iota(jnp.int32, (tq, tk), 1)

    carry_ref[...] = jnp.zeros(carry_ref.shape, F32)
    acc_ref[...] = jnp.zeros(acc_ref.shape, F32)

    heads = range(N_HEADS)
    pairs = range(N_HEADS // 2)
    pair_lanes = [slice(p * 2 * D_HEAD, (p + 1) * 2 * D_HEAD) for p in pairs]
    first_of_pair = lax.broadcasted_iota(jnp.int32, (tk, 2 * D_HEAD), 1) < D_HEAD

    def head_pair_diag(x):
        zero = jnp.zeros_like(x)
        return jnp.concatenate([jnp.where(first_of_pair, x, zero),
                                jnp.where(first_of_pair, zero, x)], axis=0)

    def step(kblks, vblks, visibles):
        nb = len(kblks)
        zs = [[None] * N_HEADS for _ in range(nb)]
        for b in range(nb):
            for p in pairs:
                z2 = _dot_nt(q_ref[0, :, pair_lanes[p]], head_pair_diag(kblks[b][:, pair_lanes[p]]))
                zs[b][2 * p], zs[b][2 * p + 1] = z2[:, :tk], z2[:, tk:]
        sps = [[None] * N_HEADS for _ in range(nb)]
        for b in range(nb):
            for h in heads:
                z = zs[b][h]
                sp = jnp.maximum(z, jnp.log(1.0 + jnp.exp(jnp.minimum(z, SOFTPLUS_CUT))))
                if visibles[b] is not None:
                    sp = jnp.where(visibles[b], sp, 0.0)
                sps[b][h] = sp.astype(BF16)
        order = list(reversed(range(nb)))
        upper = upper_of[nb]
        srs = [_dot(jnp.concatenate([sps[b][h] for b in order], axis=1), upper) for h in heads]
        run_visible = None
        if visibles[0] is not None:
            run_visible = jnp.concatenate([visibles[b] for b in order], axis=1)
        weights = []
        for h in heads:
            later = carry_ref[h]
            z_run = jnp.concatenate([zs[b][h] for b in order], axis=1)
            a = jnp.exp(z_run - srs[h] - jnp.concatenate([later] * nb, axis=1))
            if run_visible is not None:
                a = jnp.where(run_visible, a, 0.0)
            weights.append(a.astype(BF16))
            carry_ref[h] = later + jnp.broadcast_to(srs[h][:, 0:1], (tq, tk))
        for p in pairs:
            w2 = jnp.concatenate([weights[2 * p], weights[2 * p + 1]], axis=1)
            vd = [head_pair_diag(vblks[b][:, pair_lanes[p]]) for b in order]
            v2 = jnp.concatenate([v[:tk] for v in vd] + [v[tk:] for v in vd], axis=0)
            acc_ref[p] += _dot(w2, v2)

    def diag_rows(dd):
        return slice(dd * tk, (dd + 1) * tk)

    diag = list(reversed(range(n_diag)))
    step([kd_ref[0, diag_rows(dd), :].astype(BF16) for dd in diag],
         [vd_ref[0, diag_rows(dd), :].astype(BF16) for dd in diag],
         [dd * tk + kj < qi for dd in diag])

    def key_rows(blk):
        return pl.ds(pl.multiple_of(blk * tk, tk), tk)

    def body(i, c):
        newest = n_blocks - 1 - STICK_RUN * i
        blks = [newest - j for j in range(STICK_RUN)]
        step([km_ref[0, key_rows(blk), :].astype(BF16) for blk in blks],
             [vm_ref[0, key_rows(blk), :].astype(BF16) for blk in blks], [None] * STICK_RUN)
        return c

    lax.fori_loop(0, n_blocks // STICK_RUN, body, 0)
    o_ref[0] = jnp.concatenate([acc_ref[p] for p in pairs], axis=1)


def _stick(q, k_diag, v_diag, k_main, v_main, *, tq, n_q_blocks, n_main):
    b = q.shape[0]
    tm = k_main.shape[1]
    blk = lambda bb, i: (bb, i, 0)
    whole = lambda bb, i: (bb, 0, 0)
    return pl.pallas_call(
        functools.partial(_stick_kernel, n_main=n_main),
        grid=(b, n_q_blocks),
        in_specs=[pl.BlockSpec((1, tq, W_BR), blk),
                  pl.BlockSpec((1, tq, W_BR), blk),
                  pl.BlockSpec((1, tq, W_BR), blk),
                  pl.BlockSpec((1, tm, W_BR), whole),
                  pl.BlockSpec((1, tm, W_BR), whole)],
        out_specs=pl.BlockSpec((1, tq, W_BR), blk),
        out_shape=jax.ShapeDtypeStruct((b, n_q_blocks * tq, W_BR), F32),
        scratch_shapes=[pltpu.VMEM((N_HEADS, tq, KEY_BLOCK), F32),
                        pltpu.VMEM((N_HEADS // 2, tq, 2 * D_HEAD), F32)],
        compiler_params=_cparams(2),
        name="stick_breaking",
    )(q, k_diag, v_diag, k_main, v_main)


def _flip_magnitude_if_negative(word):
    return word ^ (lax.shift_right_arithmetic(word, 31) & jnp.int32(0x7FFFFFFF))


def _key_to_float(key):
    return lax.bitcast_convert_type(_flip_magnitude_if_negative(key), F32)


def _float_to_key(x):
    return _flip_magnitude_if_negative(lax.bitcast_convert_type(x, jnp.int32))


DSA_TK = 2 * KEY_BLOCK
DSA_TQ = 256
ALIBI_ROWS = 64
BISECT_GROUP = 4
BISECT_FREE_GROUPS = 3


def _dsa_kernel(biqt_ref, biwt_ref, bqt_ref, ik_ref, k_ref, vt_ref, o_ref, sc_ref,
                *, n_full, q_off, n_valid, n_sel):
    tq = bqt_ref.shape[2]
    tk = DSA_TK
    sub = tk // KEY_BLOCK
    qb = pl.program_id(1)
    n_full_blocks = (qb * tq) // tk if n_full is None else n_full
    n_blocks = n_full_blocks + 1
    iw = biwt_ref[0]
    q0 = q_off + qb * tq

    kloc = lax.broadcasted_iota(jnp.int32, (tk, tq), 0)
    qpos = q0 + lax.broadcasted_iota(jnp.int32, (tk, tq), 1)

    def score_block(kb):
        start = pl.multiple_of(kb * tk, tk)
        ikb = ik_ref[0, pl.ds(start, tk), :].astype(BF16)
        sc = jnp.zeros((tk, tq), F32)
        for hp in range(H_IDX // 2):
            iq2 = jnp.concatenate(
                [biqt_ref[0, h * D_IDX:(h + 1) * D_IDX, :] for h in (2 * hp, 2 * hp + 1)], axis=1)
            lg2 = _dot(ikb, iq2)
            for u in range(2):
                h = 2 * hp + u
                sc = sc + iw[h:h + 1, :] * jnp.maximum(lg2[:, u * tq:(u + 1) * tq], 0.0)
        return jnp.where(sc == 0.0, 0.0, sc)

    def fill(j, c):
        kbs = [jnp.minimum(2 * j + u, n_full_blocks - 1) for u in range(2)]
        scores = [score_block(kb) for kb in kbs]
        for kb, sc in zip(kbs, scores):
            sc_ref[kb] = sc
        return c

    lax.fori_loop(0, (n_full_blocks + 1) // 2, fill, 0)
    kpos_last = n_full_blocks * tk + kloc
    admissible = ((lax.shift_right_logical(kpos_last, 6) <= lax.shift_right_logical(qpos, 6))
                  & (kpos_last < n_valid))
    sc_ref[n_full_blocks] = jnp.where(admissible, score_block(n_full_blocks), -jnp.inf)
    sc_ref[n_blocks] = jnp.full((tk, tq), -jnp.inf, F32)
    n_pairs = (n_blocks + 1) // 2

    def count(pred):
        def body(j, acc):
            for u in range(2):
                for r in range(sub):
                    x = sc_ref[2 * j + u, r * KEY_BLOCK:(r + 1) * KEY_BLOCK, :]
                    acc = acc + jnp.where(pred(x), 1.0, 0.0)
            return acc
        acc = lax.fori_loop(0, n_pairs, body, jnp.zeros((KEY_BLOCK, tq), F32))
        return jnp.sum(acc, axis=0, keepdims=True)

    k_sel = jnp.float32(n_sel)

    n_ge0 = count(lambda x: x >= 0.0)
    n_gt0 = count(lambda x: x > 0.0)
    zero_thr = (n_ge0 >= k_sel) & (n_gt0 < k_sel)
    positive = n_gt0 >= k_sel
    assert tk >= n_sel
    gmax = lax.fori_loop(0, n_blocks, lambda kb, g: jnp.maximum(g, sc_ref[kb]),
                         jnp.full((tk, tq), -jnp.inf, F32))
    lo_b = jnp.maximum(_float_to_key(jnp.min(gmax, axis=0, keepdims=True)), KEY_LO)
    hi_b = jnp.minimum(_float_to_key(jnp.max(gmax, axis=0, keepdims=True)), KEY_HI - 1) + 1
    lo0 = jnp.where(zero_thr, 0, jnp.where(positive, jnp.maximum(lo_b, 1), lo_b))
    hi0 = jnp.where(zero_thr, 1, jnp.where(positive, hi_b, jnp.minimum(hi_b, 0)))

    def unresolved(lo, hi):
        return (hi - lo) != 1

    def bisect_cond(st):
        it, lo, hi = st
        return ((it < 32 // BISECT_GROUP)
                & (jnp.max(jnp.where(unresolved(lo, hi), 1.0, 0.0)) > 0.5))

    def bisect(st):
        it, lo, hi = st
        for _ in range(BISECT_GROUP):
            active = unresolved(lo, hi)
            mid = lo + lax.shift_right_logical(hi - lo, 1)
            n_ge = count(lambda x, mid=mid: x >= _key_to_float(mid))
            ge = n_ge >= k_sel
            exact = n_ge == k_sel
            lo, hi = (jnp.where(active & ge, mid, lo),
                      jnp.where(active, jnp.where(exact, mid + 1, jnp.where(ge, hi, mid)), hi))
        return it + 1, lo, hi

    st = lax.fori_loop(0, BISECT_FREE_GROUPS, lambda _, s: bisect(s), (jnp.int32(0), lo0, hi0))
    _, lo, _ = lax.while_loop(bisect_cond, bisect, st)
    thr = _key_to_float(lo)
    need = k_sel - count(lambda x: x > thr)

    srow = lax.broadcasted_iota(jnp.int32, (tk, tk), 0)
    jcol = lax.broadcasted_iota(jnp.int32, (tk, tk), 1)
    before = jnp.where(jcol < srow, 1.0, 0.0).astype(BF16)

    def select(j, seen):
        xs = [sc_ref[2 * j + u] for u in range(2)]
        eqs = [x == thr for x in xs]
        eqfs = [jnp.where(eq, 1.0, 0.0) for eq in eqs]
        within = [_dot(before, eqf.astype(BF16)) for eqf in eqfs]
        for u in range(2):
            chosen = (xs[u] > thr) | (eqs[u] & (within[u] + seen < need))
            sc_ref[2 * j + u] = jnp.where(chosen, 0.0, NEG_INF)
            seen = seen + jnp.sum(eqfs[u], axis=0, keepdims=True)
        return seen

    lax.fori_loop(0, n_pairs, select, jnp.zeros((1, tq), F32))

    q_all = jnp.concatenate(
        [bqt_ref[0, h * D_HEAD:(h + 1) * D_HEAD, :] for h in range(N_HEADS)], axis=1)
    slope = jnp.concatenate(
        [jnp.full((1, tq), 2.0 ** -(h + 1), F32) for h in range(N_HEADS)], axis=1)
    qi_all = jnp.concatenate(
        [lax.broadcasted_iota(jnp.int32, (1, tq), 1)] * N_HEADS, axis=1).astype(F32)
    arow = lax.broadcasted_iota(jnp.int32, (ALIBI_ROWS, N_HEADS * tq), 0)
    q_extra = jnp.where(arow == 0, -slope * qi_all, jnp.where(arow <= 2, slope, 0.0))
    q_aug = jnp.concatenate([q_all, q_extra.astype(BF16)], axis=0)
    acol = lax.broadcasted_iota(jnp.int32, (tk, ALIBI_ROWS), 1)
    kj_f = lax.broadcasted_iota(jnp.int32, (tk, ALIBI_ROWS), 0).astype(F32)

    def load_kv(kb):
        start = pl.multiple_of(kb * tk, tk)
        kblk = k_ref[0, pl.ds(start, tk), :].astype(BF16)
        vblk_t = jnp.concatenate(
            [vt_ref[0, kb * sub + j].astype(BF16) for j in range(sub)], axis=1)
        return kblk, vblk_t

    def softmax_step(st, s_heads, vblk_t):
        ms, ls, accs = st
        m_new, new_l, new_acc = [], [], []
        for hp in range(N_HEADS // 2):
            hs = (2 * hp, 2 * hp + 1)
            m2 = [jnp.maximum(ms[h], jnp.max(s_heads[h], axis=0, keepdims=True)) for h in hs]
            p2 = [jnp.exp(s_heads[h] - m2[u]) for u, h in enumerate(hs)]
            pv2 = _dot(vblk_t, jnp.concatenate([p.astype(BF16) for p in p2], axis=1))
            for u, h in enumerate(hs):
                alpha = jnp.exp(ms[h] - m2[u])
                m_new.append(m2[u])
                new_l.append(alpha * ls[h] + jnp.sum(p2[u], axis=0, keepdims=True))
                new_acc.append(alpha * accs[h] + pv2[:, u * tq:(u + 1) * tq])
        return tuple(m_new), tuple(new_l), tuple(new_acc)

    def per_head_logits(keys, queries):
        tiles = []
        for hp in range(N_HEADS // 2):
            s2 = _dot(keys, queries[:, 2 * hp * tq:(2 * hp + 2) * tq])
            tiles += [s2[:, :tq], s2[:, tq:]]
        return tiles

    def full_block_logits(kb, mb):
        kblk, vblk_t = load_kv(kb)
        neg_delta = (kb * tk - q0).astype(F32)
        k_extra = jnp.where(acol == 0, 1.0,
                            jnp.where(acol == 1, kj_f, jnp.where(acol == 2, neg_delta, 0.0)))
        k_aug = jnp.concatenate([kblk, k_extra.astype(BF16)], axis=1)
        return [s + mb for s in per_head_logits(k_aug, q_aug)], vblk_t

    def attend_pair(parts, st):
        s_heads = [jnp.concatenate([parts[0][0][h], parts[1][0][h]], axis=0)
                   for h in range(N_HEADS)]
        return softmax_step(st, s_heads, jnp.concatenate([parts[0][1], parts[1][1]], axis=1))

    def attend_full_pair(j, st):
        return attend_pair([full_block_logits(2 * j + u, sc_ref[2 * j + u]) for u in range(2)], st)

    st = lax.fori_loop(
        0, n_full_blocks // 2, attend_full_pair,
        (tuple(jnp.full((1, tq), -3e38, F32) for _ in range(N_HEADS)),
         tuple(jnp.zeros((1, tq), F32) for _ in range(N_HEADS)),
         tuple(jnp.zeros((DKV_B, tq), F32) for _ in range(N_HEADS))))

    odd = (n_full_blocks % 2) == 1
    spare = jnp.maximum(n_full_blocks - 1, 0)
    spare_part = full_block_logits(spare, jnp.where(odd, sc_ref[spare], NEG_INF))
    kblk, vblk_t = load_kv(n_full_blocks)
    dist = jnp.abs(qpos - kpos_last).astype(F32)
    mb = sc_ref[n_full_blocks]
    last_part = ([s + (mb - (2.0 ** -(h + 1)) * dist)
                  for h, s in enumerate(per_head_logits(kblk, q_all))], vblk_t)
    _, ls, accs = attend_pair([spare_part, last_part], st)

    o_ref[0] = jnp.concatenate([(accs[h] / ls[h]).T for h in range(N_HEADS)], axis=1)


def _dsa(biqt, biwt, bqt, ik, k, vt, *, tq, n_q_blocks, n_full, q_off, n_valid, n_sel):
    b = bqt.shape[0]
    tk_total = ik.shape[1]
    colmaj = lambda bb, i: (bb, 0, i)
    whole3 = lambda bb, i: (bb, 0, 0)
    return pl.pallas_call(
        functools.partial(_dsa_kernel, n_full=n_full, q_off=q_off, n_valid=n_valid, n_sel=n_sel),
        grid=(b, n_q_blocks),
        in_specs=[pl.BlockSpec((1, H_IDX * D_IDX, tq), colmaj),
                  pl.BlockSpec((1, biwt.shape[1], tq), colmaj),
                  pl.BlockSpec((1, W_BR, tq), colmaj),
                  pl.BlockSpec((1, tk_total, D_IDX), whole3),
                  pl.BlockSpec((1, tk_total, DKV_B), whole3),
                  pl.BlockSpec((1, tk_total // KEY_BLOCK, DKV_B, KEY_BLOCK),
                               lambda bb, i: (bb, 0, 0, 0))],
        out_specs=pl.BlockSpec((1, tq, W_BR), lambda bb, i: (bb, i, 0)),
        out_shape=jax.ShapeDtypeStruct((b, n_q_blocks * tq, W_BR), F32),
        scratch_shapes=[pltpu.VMEM((tk_total // DSA_TK + 1, DSA_TK, tq), F32)],
        compiler_params=_cparams(2),
        name="dsa",
    )(biqt, biwt, bqt, ik, k, vt)


def _band_kernel(*refs, n_win, win_start_blocks, n_valid_win):
    q_ref = refs[0]
    k_refs = refs[1:1 + n_win]
    v_refs = refs[1 + n_win:1 + 2 * n_win]
    bias_ref = refs[1 + 2 * n_win]
    o_ref = refs[2 + 2 * n_win]
    tq = q_ref.shape[1]
    w = n_win * tq
    i = pl.program_id(1)
    col = lax.broadcasted_iota(jnp.int32, (1, w), 1)
    if win_start_blocks is None:
        valid = col < n_valid_win
    else:
        valid = ((i + win_start_blocks) * tq + col >= 0) & (col < n_valid_win)
    kwin = jnp.concatenate([r[0] for r in k_refs], axis=0)
    vwin = jnp.concatenate([r[0] for r in v_refs], axis=0)
    outs = []
    for h in range(N_HEADS):
        lanes = slice(h * D_HEAD, (h + 1) * D_HEAD)
        s = _dot_nt(q_ref[0, :, lanes], kwin[:, lanes]) + bias_ref[h]
        s = jnp.where(valid, s, NEG_INF)
        e = jnp.exp(s - jnp.max(s, axis=-1, keepdims=True))
        den = jnp.sum(e, axis=-1, keepdims=True)
        outs.append(_dot(e.astype(BF16), vwin[:, lanes]) / den)
    o_ref[0] = jnp.concatenate(outs, axis=1)


def _band(q, k, v, bias, *, tq, n_q_blocks, sliding, n_valid_win):
    b = q.shape[0]
    n_win = bias.shape[2] // tq
    assert bias.shape[1] == tq

    def kv_map(wi):
        if sliding:
            return lambda bb, i: (bb, jnp.maximum(i - (n_win - 1) + wi, 0), 0)
        return lambda bb, i: (bb, wi, 0)

    kv_specs = [pl.BlockSpec((1, tq, W_BR), kv_map(wi)) for wi in range(n_win)]
    return pl.pallas_call(
        functools.partial(_band_kernel, n_win=n_win,
                          win_start_blocks=-(n_win - 1) if sliding else None,
                          n_valid_win=n_valid_win),
        grid=(b, n_q_blocks),
        in_specs=([pl.BlockSpec((1, tq, W_BR), lambda bb, i: (bb, i, 0))] + kv_specs + kv_specs
                  + [pl.BlockSpec(bias.shape, lambda bb, i: (0, 0, 0))]),
        out_specs=pl.BlockSpec((1, tq, W_BR), lambda bb, i: (bb, i, 0)),
        out_shape=jax.ShapeDtypeStruct((b, n_q_blocks * tq, W_BR), F32),
        compiler_params=_cparams(2),
        name="band",
    )(q, *([k] * n_win), *([v] * n_win), bias)


def _merge_kernel(x_ref, mod_ref, ya_ref, yb_ref, yc_ref, wz_ref, wg_ref, bg_ref, wbr_ref,
                  wo_ref, lng_ref, lnb_ref, o_ref, *, alpha):
    x = x_ref[0]
    d = x.shape[-1]
    mod = mod_ref[0]
    shift = mod[:, :d]
    scale = mod[:, d:2 * d]
    gate = mod[:, 2 * d:3 * d]
    h = (_layer_norm(x) * (1.0 + scale) + shift).astype(BF16)
    m = jnp.zeros(x.shape, F32)
    for n, y_ref in enumerate((ya_ref, yb_ref, yc_ref)):
        z = _dot(h, wz_ref[:, n * W_BR:(n + 1) * W_BR])
        u = (y_ref[0] * (z * _sigmoid(z))).astype(BF16)
        t = _dot(u, wbr_ref[n])
        g = _sigmoid(_dot(h, wg_ref[n]) + bg_ref[n])
        m = m + g * t
    out = _dot(m.astype(BF16), wo_ref[...])
    r = alpha * x + gate * out
    o_ref[0] = _layer_norm(r) * lng_ref[...] + lnb_ref[...]


def _merge(x, mod, ya, yb, yc, wz, wg, bg, wbr, wo, lng, lnb, *, tm, alpha):
    b, t, d = x.shape
    row = lambda bb, i: (bb, i, 0)
    const2 = lambda bb, i: (0, 0)
    const3 = lambda bb, i: (0, 0, 0)
    return pl.pallas_call(
        functools.partial(_merge_kernel, alpha=alpha),
        grid=(b, t // tm),
        in_specs=[pl.BlockSpec((1, tm, d), row),
                  pl.BlockSpec((1, 1, mod.shape[-1]), lambda bb, i: (bb, 0, 0)),
                  pl.BlockSpec((1, tm, W_BR), row),
                  pl.BlockSpec((1, tm, W_BR), row),
                  pl.BlockSpec((1, tm, W_BR), row),
                  pl.BlockSpec(wz.shape, const2),
                  pl.BlockSpec(wg.shape, const3),
                  pl.BlockSpec(bg.shape, const3),
                  pl.BlockSpec(wbr.shape, const3),
                  pl.BlockSpec(wo.shape, const2),
                  pl.BlockSpec(lng.shape, const2),
                  pl.BlockSpec(lnb.shape, const2)],
        out_specs=pl.BlockSpec((1, tm, d), row),
        out_shape=jax.ShapeDtypeStruct((b, t, d), F32),
        compiler_params=_cparams(2),
        name="merge",
    )(x, mod, ya, yb, yc, wz, wg, bg, wbr, wo, lng, lnb)


def _split_w_in(w):
    sizes = (W_BR, W_BR, W_BR, W_BR, W_BR, DKV_B, DKV_B, W_BR, H_IDX * D_IDX, D_IDX, H_IDX,
             W_BR, W_BR, W_BR, W_BR)
    cols, off = [], 0
    for s in sizes:
        cols.append(w[:, off:off + s])
        off += s
    aq, ak, av, az, bq, bk, bv, bz, biq, bik, biw, cq, ck, cv, cz = cols
    d = w.shape[0]
    w1 = jnp.concatenate([aq * ATT_SCALE, ak, av, cq * ATT_SCALE, ck, cv], axis=1).astype(BF16)
    w2 = jnp.concatenate([bk, bv, bik, jnp.zeros((d, N_W2 - 3 * DKV_B), w.dtype)],
                         axis=1).astype(BF16)
    wt = jnp.concatenate([bq * ATT_SCALE, biq * (D_IDX ** -0.5), bv, biw * IDX_HEAD_SCALE,
                          jnp.zeros((d, N_WT - ROW_BIW - H_IDX), w.dtype)], axis=1).T.astype(BF16)
    wz = jnp.concatenate([az, bz, cz], axis=1).astype(BF16)
    return w1, w2, wt, wz


def _key_blocks_t(v):
    b, t, n = v.shape
    return jnp.transpose(v.reshape(b, t // KEY_BLOCK, KEY_BLOCK, n), (0, 1, 3, 2))


def kernel(x_prompt, x_sample, c_prompt, c_sample, cache_a_k, cache_a_v, cache_b_k, cache_b_v,
           cache_b_kidx, cache_c_k, cache_c_v, w_ada, b_ada, w_in, w_gate, b_gate, w_branch,
           w_out, rel_bias, ln_g, ln_b):
    depth = w_in.shape[0]
    bp, seq, d = x_prompt.shape
    bs, dec_seq, _ = x_sample.shape
    past_len = cache_a_k.shape[2]
    band_rows = cache_c_k.shape[2]
    alpha = (2.0 * depth) ** 0.25
    assert seq % 512 == 0 and dec_seq <= CHUNK and past_len % DSA_TK == 0
    assert band_rows == BAND_ROWS and past_len // CHUNK == (past_len + dec_seq - 1) // CHUNK

    mod_all = _modulation(jnp.concatenate([c_prompt, c_sample], axis=0), w_ada, b_ada)
    bias_all = _band_bias(rel_bias)

    xp = x_prompt
    xs = jnp.pad(x_sample, ((0, 0), (0, SAMPLE_PAD - dec_seq), (0, 0)))
    n_sel_p = min(TOPK_MAX, seq // 4)
    n_sel_s = min(TOPK_MAX, (past_len + dec_seq) // 4)
    tm_p = 512
    st_p = [[] for _ in range(5)]
    st_s = [[] for _ in range(7)]
    ak_layers, av_layers = [], []

    for l in range(depth):
        w1, w2, wt, wz = _split_w_in(w_in[l])
        wg = w_gate[l].astype(BF16)
        bg = b_gate[l].reshape(N_BRANCH, 1, d)
        wbr = w_branch[l].astype(BF16)
        wo = w_out[l].astype(BF16)
        lng = ln_g[l].reshape(1, d)
        lnb = ln_b[l].reshape(1, d)
        mod_p = mod_all[l, :bp].reshape(bp, 1, 3 * d)
        mod_s = mod_all[l, bp:].reshape(bs, 1, 3 * d)
        bias_p = bias_all[l]

        gather = (tuple(ak_layers), tuple(av_layers)) if l == depth - 1 and l > 0 else None
        (aq, ak, av, akb, avb, cq, ckb, cvb, ckt, cvt, bk, bv, bik, bkb, bikb,
         bqt, biqt, bvt, biwt) = _in_proj(xp, mod_p, w1, w2, wt, tm_p, gather_layers=gather)
        ak_layers.append(ak)
        av_layers.append(av)
        ya = _stick(aq, akb, avb, akb, avb, tq=STICK_TQ, n_q_blocks=seq // STICK_TQ, n_main=None)
        yb = _dsa(biqt, biwt, bqt, bikb, bkb, bvt, tq=DSA_TQ, n_q_blocks=seq // DSA_TQ,
                  n_full=None, q_off=0, n_valid=seq, n_sel=n_sel_p)
        yc = _band(cq, ckb, cvb, bias_p, tq=BAND_TQ, n_q_blocks=seq // BAND_TQ, sliding=True,
                   n_valid_win=BAND_W)
        xp = _merge(xp, mod_p, ya, yb, yc, wz, wg, bg, wbr, wo, lng, lnb, tm=512, alpha=alpha)
        for lst, val in zip(st_p, (bk, bv, bik, ckt.reshape(bp, -1, N_HEADS, D_HEAD),
                                   cvt.reshape(bp, -1, N_HEADS, D_HEAD))):
            lst.append(val)

        (aq, ak, av, akb, avb, cq, ckb, cvb, ckt, cvt, bk, bv, bik, bkb, bikb,
         bqt, biqt, bvt, biwt) = _in_proj(xs, mod_s, w1, w2, wt, SAMPLE_PAD)
        ya = _stick(aq, akb, avb, cache_a_k[l].reshape(bs, past_len, W_BR),
                    cache_a_v[l].reshape(bs, past_len, W_BR),
                    tq=SAMPLE_PAD, n_q_blocks=1, n_main=past_len // KEY_BLOCK)
        key_pad = jnp.zeros((bs, DSA_TK - SAMPLE_PAD, DKV_B), BF16)
        ik_cat = jnp.concatenate([cache_b_kidx[l].astype(BF16), bikb, key_pad], axis=1)
        k_cat = jnp.concatenate([cache_b_k[l].astype(BF16), bkb, key_pad], axis=1)
        vt_cat = jnp.concatenate([_key_blocks_t(cache_b_v[l].astype(BF16)), bvt,
                                  _key_blocks_t(key_pad)], axis=1)
        yb = _dsa(biqt, biwt, bqt, ik_cat, k_cat, vt_cat, tq=SAMPLE_PAD, n_q_blocks=1,
                  n_full=past_len // DSA_TK, q_off=past_len, n_valid=past_len + dec_seq,
                  n_sel=n_sel_s)
        kc_cat = jnp.concatenate(
            [cache_c_k[l].reshape(bs, band_rows, W_BR).astype(BF16), ckb], axis=1)
        vc_cat = jnp.concatenate(
            [cache_c_v[l].reshape(bs, band_rows, W_BR).astype(BF16), cvb], axis=1)
        yc = _band(cq, kc_cat, vc_cat, bias_p[:, :SAMPLE_PAD, :band_rows + SAMPLE_PAD],
                   tq=SAMPLE_PAD, n_q_blocks=1, sliding=False, n_valid_win=band_rows + dec_seq)
        xs = _merge(xs, mod_s, ya, yb, yc, wz, wg, bg, wbr, wo, lng, lnb, tm=SAMPLE_PAD,
                    alpha=alpha)
        new_ck = jnp.concatenate(
            [cache_c_k[l], ckt[:, :dec_seq].reshape(bs, dec_seq, N_HEADS, D_HEAD)],
            axis=1)[:, -band_rows:]
        new_cv = jnp.concatenate(
            [cache_c_v[l], cvt[:, :dec_seq].reshape(bs, dec_seq, N_HEADS, D_HEAD)],
            axis=1)[:, -band_rows:]
        for lst, val in zip(st_s, (ak[:, :dec_seq].reshape(bs, dec_seq, N_HEADS, D_HEAD),
                                   av[:, :dec_seq].reshape(bs, dec_seq, N_HEADS, D_HEAD),
                                   bk[:, :dec_seq], bv[:, :dec_seq], bik[:, :dec_seq],
                                   new_ck, new_cv)):
            lst.append(val)

    return (xp, xs[:, :dec_seq],
            ak_layers[-1].reshape(depth, bp, seq, N_HEADS, D_HEAD),
            av_layers[-1].reshape(depth, bp, seq, N_HEADS, D_HEAD),
            *[jnp.stack(v) for v in st_p], *[jnp.stack(v) for v in st_s])
```
